```python
import math
import jax, jax.numpy as jnp
from jax import lax
import numpy as np

D_MODEL = 2048
BATCH = 4
SEQ = 2048
DEPTH = 2

F32 = jnp.float32
HEAD_DIM = 128
GDN_HEADS = 6
GDN_WIDTH = GDN_HEADS * HEAD_DIM
CONV_WIDTH = 5
GDN_CHUNK = 64
MLSTM_HEADS = 6
MLSTM_WIDTH = MLSTM_HEADS * HEAD_DIM
MLSTM_CHUNK = 64
SWA_GROUPS = ((128, 1), (512, 4), (2048, 16))
SWA_HEADS_PER_GROUP = 2
SWA_HEADS = SWA_HEADS_PER_GROUP * len(SWA_GROUPS)
SWA_WIDTH = SWA_HEADS * HEAD_DIM
SWA_BLOCK = 64
ROPE_THETA = 500000.0
ROPE_DIM = HEAD_DIM // 4
N_BRANCH = 3
BRANCH_WIDTH = GDN_WIDTH
N_EXPERTS = 16
EXPERT_FF = 1024
EC_CAPACITY_FACTOR = 2
RMS_EPS = 1e-6
NEG_INF = -1e30
IN_SIZES = (3 * GDN_WIDTH, GDN_WIDTH, 4 * GDN_HEADS,
            3 * MLSTM_WIDTH, MLSTM_WIDTH, 4 * MLSTM_HEADS,
            3 * SWA_WIDTH, N_BRANCH * D_MODEL)
D_IN = sum(IN_SIZES)

kernel_name = 'hybrid_gdn_mlstm_dilated_ec_encoder'


def rms_norm(x, g):
    xf = x.astype(F32)
    y = xf * lax.rsqrt(jnp.mean(xf * xf, axis=-1, keepdims=True) + RMS_EPS)
    return (y * g.astype(F32)).astype(x.dtype)


def l2_normalize(x):
    xf = x.astype(F32)
    return xf * lax.rsqrt(jnp.sum(xf * xf, axis=-1, keepdims=True) + RMS_EPS)


def centred_depthwise_conv(x, w):
    pad = (w.shape[0] - 1) // 2
    return lax.conv_general_dilated(
        x, w[:, None, :].astype(x.dtype), window_strides=(1,), padding=[(pad, pad)],
        dimension_numbers=('NWC', 'WIO', 'NWC'), feature_group_count=x.shape[-1])


def to_chunks(t, chunk):
    n, s, h = t.shape[:3]
    t = t.reshape((n, s // chunk, chunk, h) + t.shape[3:])
    return jnp.moveaxis(t, 3, 1)


def from_chunks(o):
    nc, n, h, c, d = o.shape
    return jnp.transpose(o, (1, 0, 3, 2, 4)).reshape(n, nc * c, h, d)


def both_directions(fw, bw):
    return jnp.concatenate([fw, jnp.flip(bw, axis=1)], axis=0)


def merge_directions(o, bsz):
    return o[:bsz] + jnp.flip(o[bsz:], axis=1)


def gated_delta_rule_chunked(q, k, v, g, beta):
    L = GDN_CHUNK
    n, s, h, dk = q.shape
    dv = v.shape[-1]
    q, k, v = [to_chunks(t.astype(F32), L) for t in (q, k, v)]
    g, beta = to_chunks(g.astype(F32), L), to_chunks(beta.astype(F32), L)
    gc = jnp.cumsum(g, axis=-1)
    tri_incl = jnp.tril(jnp.ones((L, L), bool))
    tri_strict = jnp.tril(jnp.ones((L, L), bool), -1)
    diff = gc[..., :, None] - gc[..., None, :]
    decay = jnp.where(tri_incl, jnp.exp(jnp.where(tri_incl, diff, 0.0)), 0.0)
    kk = jnp.einsum('nhcid,nhcjd->nhcij', k, k)
    a_mat = jnp.eye(L, dtype=F32) + jnp.where(tri_strict, beta[..., :, None] * decay * kk, 0.0)
    rhs = jnp.concatenate([v * beta[..., None], k * (beta * jnp.exp(gc))[..., None]], axis=-1)
    sol = lax.linalg.triangular_solve(a_mat, rhs, left_side=True, lower=True, unit_diagonal=True)
    w_v, w_k = sol[..., :dv], sol[..., dv:]
    qk = jnp.einsum('nhcid,nhcjd->nhcij', q, k) * decay
    q_dec = q * jnp.exp(gc)[..., None]
    k_dec = k * jnp.exp(gc[..., -1:] - gc)[..., None]
    chunk_decay = jnp.exp(gc[..., -1])

    def step(state, xs):
        w_v_c, w_k_c, qk_c, q_dec_c, k_dec_c, cd_c = xs
        u = w_v_c - jnp.einsum('nhld,nhde->nhle', w_k_c, state)
        o = jnp.einsum('nhld,nhde->nhle', q_dec_c, state) + jnp.einsum('nhij,nhje->nhie', qk_c, u)
        state = cd_c[..., None, None] * state + jnp.einsum('nhld,nhle->nhde', k_dec_c, u)
        return state, o

    xs = [jnp.moveaxis(t, 2, 0) for t in (w_v, w_k, qk, q_dec, k_dec, chunk_decay)]
    _, o = lax.scan(step, jnp.zeros((n, h, dk, dv), F32), xs)
    return from_chunks(o)


def mlstm_chunked(q, k, v, i_pre, log_f):
    L = MLSTM_CHUNK
    n, s, h, d = q.shape
    q, k, v = [to_chunks(t.astype(F32), L) for t in (q, k, v)]
    k = k * (d ** -0.5)
    i_pre, log_f = to_chunks(i_pre.astype(F32), L), to_chunks(log_f.astype(F32), L)
    b = jnp.cumsum(log_f, axis=-1)
    tri_incl = jnp.tril(jnp.ones((L, L), bool))
    log_d = jnp.where(tri_incl, b[..., :, None] - b[..., None, :] + i_pre[..., None, :], NEG_INF)
    row_max = jnp.max(log_d, axis=-1)
    qk = jnp.einsum('nhcid,nhcjd->nhcij', q, k)
    log_end = b[..., -1:] - b + i_pre
    end_max = jnp.max(log_end, axis=-1)
    b_end = b[..., -1]

    def step(carry, xs):
        c, nvec, m0 = carry
        q_c, k_c, v_c, b_c, logd_c, rmax_c, qk_c, lend_c, emax_c, bend_c = xs
        m_t = jnp.maximum(b_c + m0[..., None], rmax_c)
        inter = jnp.exp(b_c + m0[..., None] - m_t)
        dmat = jnp.exp(logd_c - m_t[..., None]) * qk_c
        num = inter[..., None] * jnp.einsum('nhld,nhde->nhle', q_c, c) + jnp.einsum('nhij,nhje->nhie', dmat, v_c)
        den = inter * jnp.einsum('nhld,nhd->nhl', q_c, nvec) + jnp.sum(dmat, axis=-1)
        h_c = num / jnp.maximum(jnp.abs(den), jnp.exp(-m_t))[..., None]
        m_new = jnp.maximum(bend_c + m0, emax_c)
        carry_scale = jnp.exp(bend_c + m0 - m_new)
        wgt = jnp.exp(lend_c - m_new[..., None])
        c = carry_scale[..., None, None] * c + jnp.einsum('nhld,nhle->nhde', k_c * wgt[..., None], v_c)
        nvec = carry_scale[..., None] * nvec + jnp.einsum('nhl,nhld->nhd', wgt, k_c)
        return (c, nvec, m_new), h_c

    xs = [jnp.moveaxis(t, 2, 0) for t in (q, k, v, b, log_d, row_max, qk, log_end, end_max, b_end)]
    init = (jnp.zeros((n, h, d, d), F32), jnp.zeros((n, h, d), F32), jnp.zeros((n, h), F32))
    _, o = lax.scan(step, init, xs)
    return from_chunks(o)


def partial_rotary(x, pos):
    half = ROPE_DIM // 2
    inv_freq = jnp.power(ROPE_THETA, -jnp.arange(half, dtype=F32) * (2.0 / ROPE_DIM))
    ang = pos.astype(F32)[:, None] * inv_freq[None, :]
    cos = jnp.cos(ang)[None, :, None, :]
    sin = jnp.sin(ang)[None, :, None, :]
    xf = x.astype(F32)
    x1, x2, rest = xf[..., :half], xf[..., half:ROPE_DIM], xf[..., ROPE_DIM:]
    return jnp.concatenate([x1 * cos - x2 * sin, x2 * cos + x1 * sin, rest], axis=-1)


def banded_attention(q, k, v, half):
    n, t, h, d = q.shape
    blk = SWA_BLOCK
    nb = -(-t // blk)
    pad = nb * blk - t
    qb = jnp.pad(q.astype(F32), ((0, 0), (0, pad), (0, 0), (0, 0))).reshape(n, nb, blk, h, d)

    def windows(x):
        x = jnp.pad(x.astype(F32), ((0, 0), (blk, pad + blk), (0, 0), (0, 0))).reshape(n, nb + 2, blk, h, d)
        return jnp.concatenate([x[:, :-2], x[:, 1:-1], x[:, 2:]], axis=2)

    kw, vw = windows(k), windows(v)
    qpos = jnp.arange(nb * blk).reshape(nb, blk)
    kpos = (jnp.arange(nb)[:, None] - 1) * blk + jnp.arange(3 * blk)[None, :]
    rel = qpos[:, :, None] - kpos[:, None, :]
    mask = (jnp.abs(rel) <= half) & (kpos[:, None, :] >= 0) & (kpos[:, None, :] < t)
    scores = jnp.einsum('nbqhd,nbkhd->nbhqk', qb, kw)
    scores = jnp.where(mask[None, :, None], scores, NEG_INF)
    m = jnp.max(scores, axis=-1, keepdims=True)
    p = jnp.exp(scores - m)
    den = jnp.sum(p, axis=-1, keepdims=True)
    o = jnp.einsum('nbhqk,nbkhd->nbqhd', p / den, vw).reshape(n, nb * blk, h, d)[:, :t]
    lse = (m + jnp.log(den))[..., 0]
    lse = jnp.swapaxes(lse, 2, 3).reshape(n, nb * blk, h)[:, :t]
    return o, lse


def dilated_group_attention(q, k, v, half, dilation):
    bsz, s, h, d = q.shape
    t = s // dilation

    def split(x):
        return x.reshape(bsz, t, dilation, h, d).transpose(0, 2, 1, 3, 4).reshape(bsz * dilation, t, h, d)

    o, lse = banded_attention(split(q), split(k), split(v), half)
    o = o.reshape(bsz, dilation, t, h, d).transpose(0, 2, 1, 3, 4).reshape(bsz, s, h, d)
    lse = lse.reshape(bsz, dilation, t, h).transpose(0, 2, 1, 3).reshape(bsz, s, h)
    return o, lse


def gated_deltanet_branch(qkv, z, gate_pre, conv_w, a_log, dt_bias, norm_g):
    bsz, s, _ = qkv.shape
    qkv = jax.nn.silu(centred_depthwise_conv(qkv, conv_w))
    q, k, v = [t.reshape(bsz, s, GDN_HEADS, HEAD_DIM) for t in jnp.split(qkv, 3, axis=-1)]
    q = l2_normalize(q) * (HEAD_DIM ** -0.5)
    k = l2_normalize(k)
    v = v.astype(F32)
    gate_pre = gate_pre.astype(F32).reshape(bsz, s, 2, 2, GDN_HEADS)
    log_decay = -jnp.exp(a_log.astype(F32)) * jax.nn.softplus(gate_pre[:, :, 0] + dt_bias.astype(F32))
    beta = jax.nn.sigmoid(gate_pre[:, :, 1])
    o = gated_delta_rule_chunked(
        both_directions(q, q), both_directions(k, k), both_directions(v, v),
        both_directions(log_decay[:, :, 0], log_decay[:, :, 1]),
        both_directions(beta[:, :, 0], beta[:, :, 1]))
    o = merge_directions(o, bsz)
    o = rms_norm(o, norm_g) * jax.nn.silu(z.astype(F32).reshape(bsz, s, GDN_HEADS, HEAD_DIM))
    return o.reshape(bsz, s, GDN_WIDTH)


def mlstm_branch(qkv, o_pre, gate_pre, i_bias, f_bias, norm_g):
    bsz, s, _ = qkv.shape
    q, k, v = [t.astype(F32).reshape(bsz, s, MLSTM_HEADS, HEAD_DIM) for t in jnp.split(qkv, 3, axis=-1)]
    gate_pre = gate_pre.astype(F32).reshape(bsz, s, 2, 2, MLSTM_HEADS)
    i_pre = gate_pre[:, :, 0] + i_bias.astype(F32)
    log_f = jax.nn.log_sigmoid(gate_pre[:, :, 1] + f_bias.astype(F32))
    hc = mlstm_chunked(
        both_directions(q, q), both_directions(k, k), both_directions(v, v),
        both_directions(i_pre[:, :, 0], i_pre[:, :, 1]),
        both_directions(log_f[:, :, 0], log_f[:, :, 1]))
    hc = merge_directions(hc, bsz)
    hc = jax.nn.sigmoid(o_pre.astype(F32)).reshape(bsz, s, MLSTM_HEADS, HEAD_DIM) * hc
    return rms_norm(hc, norm_g).reshape(bsz, s, MLSTM_WIDTH)


def dilated_attention_branch(qkv, pos):
    bsz, s, _ = qkv.shape
    q, k, v = [t.reshape(bsz, s, SWA_HEADS, HEAD_DIM) for t in jnp.split(qkv, 3, axis=-1)]
    q = partial_rotary(q, pos) * (HEAD_DIM ** -0.5)
    k = partial_rotary(k, pos)
    outs, lses = [], []
    for gi, (window, dilation) in enumerate(SWA_GROUPS):
        hs = slice(gi * SWA_HEADS_PER_GROUP, (gi + 1) * SWA_HEADS_PER_GROUP)
        o, lse = dilated_group_attention(q[:, :, hs], k[:, :, hs], v[:, :, hs], window // (2 * dilation), dilation)
        outs.append(o)
        lses.append(lse)
    weights = jax.nn.softmax(jnp.stack(lses, axis=0), axis=0)
    y = jnp.concatenate([weights[gi][..., None] * outs[gi] for gi in range(len(SWA_GROUPS))], axis=2)
    return y.reshape(bsz, s, SWA_WIDTH)


def hybrid_mixer(h, pos, w_in, conv_w, gdn_a_log, gdn_dt_bias, gdn_norm_g,
                 mlstm_i_bias, mlstm_f_bias, mlstm_norm_g, w_branch, w_out):
    bsz, s, _ = h.shape
    split_at = np.cumsum(IN_SIZES)[:-1].tolist()
    a_qkv, a_z, a_gate, b_qkv, b_o, b_gate, c_qkv, merge_pre = jnp.split(h @ w_in, split_at, axis=-1)
    y_a = gated_deltanet_branch(a_qkv, a_z, a_gate, conv_w, gdn_a_log, gdn_dt_bias, gdn_norm_g)
    y_b = mlstm_branch(b_qkv, b_o, b_gate, mlstm_i_bias, mlstm_f_bias, mlstm_norm_g)
    y_c = dilated_attention_branch(c_qkv, pos)
    ys = jnp.stack([y_a, y_b, y_c], axis=2).astype(h.dtype)
    branch = jnp.einsum('bsnw,nwd->bsnd', ys, w_branch)
    gates = jax.nn.sigmoid(merge_pre.reshape(bsz, s, N_BRANCH, D_MODEL))
    return jnp.sum(gates * branch, axis=2) @ w_out


def expert_choice_moe(h, w_router, w_gate, w_up, w_down):
    bsz, s, _ = h.shape
    cap = EC_CAPACITY_FACTOR * s // N_EXPERTS
    affinity = jax.nn.softmax(jnp.einsum('bsd,de->bse', h, w_router).astype(F32), axis=-1)
    gate, idx = lax.top_k(jnp.swapaxes(affinity, 1, 2), cap)
    bidx = jnp.arange(bsz)[:, None, None]
    xe = h[bidx, idx]
    hid = jax.nn.silu(jnp.einsum('becd,edf->becf', xe, w_gate)) * jnp.einsum('becd,edf->becf', xe, w_up)
    ye = jnp.einsum('becf,efd->becd', hid, w_down) * gate[..., None].astype(h.dtype)
    return jnp.zeros_like(h).at[bidx, idx].add(ye)


def setup_inputs(seed: int = 0) -> dict:
    key = jax.random.key(seed)
    ks = jax.random.split(key, 18)
    L = DEPTH

    def normal(k, shape, scale):
        return jax.random.normal(k, shape, F32) * scale

    def gain(k, shape):
        return 1.0 + 0.02 * jax.random.normal(k, shape, F32)

    x = jax.random.normal(ks[0], (BATCH, SEQ, D_MODEL), F32)
    norm1_g = gain(ks[1], (L, D_MODEL))
    w_in = normal(ks[2], (L, D_MODEL, D_IN), D_MODEL ** -0.5)
    conv_w = normal(ks[3], (L, CONV_WIDTH, 3 * GDN_WIDTH), CONV_WIDTH ** -0.5)
    gdn_a_log = jnp.log(jax.random.uniform(ks[4], (L, 2, GDN_HEADS), F32, 1.0, 16.0))
    dt = jnp.exp(jax.random.uniform(ks[5], (L, 2, GDN_HEADS), F32, math.log(1e-3), math.log(1e-1)))
    gdn_dt_bias = dt + jnp.log(-jnp.expm1(-dt))
    gdn_norm_g = gain(ks[6], (L, HEAD_DIM))
    mlstm_i_bias = normal(ks[7], (L, 2, MLSTM_HEADS), 0.1)
    mlstm_f_bias = jnp.linspace(3.0, 6.0, MLSTM_HEADS, dtype=F32) + normal(ks[8], (L, 2, MLSTM_HEADS), 0.1)
    mlstm_norm_g = gain(ks[9], (L, MLSTM_HEADS, HEAD_DIM))
    w_branch = normal(ks[10], (L, N_BRANCH, BRANCH_WIDTH, D_MODEL), BRANCH_WIDTH ** -0.5)
    w_out = normal(ks[11], (L, D_MODEL, D_MODEL), D_MODEL ** -0.5)
    norm2_g = gain(ks[12], (L, D_MODEL))
    w_router = normal(ks[13], (L, D_MODEL, N_EXPERTS), D_MODEL ** -0.5)
    w_gate = normal(ks[14], (L, N_EXPERTS, D_MODEL, EXPERT_FF), D_MODEL ** -0.5)
    w_up = normal(ks[15], (L, N_EXPERTS, D_MODEL, EXPERT_FF), D_MODEL ** -0.5)
    w_down = normal(ks[16], (L, N_EXPERTS, EXPERT_FF, D_MODEL), EXPERT_FF ** -0.5)
    final_norm_g = gain(ks[17], (D_MODEL,))
    return {'x': x, 'norm1_g': norm1_g, 'w_in': w_in, 'conv_w': conv_w,
            'gdn_a_log': gdn_a_log, 'gdn_dt_bias': gdn_dt_bias, 'gdn_norm_g': gdn_norm_g,
            'mlstm_i_bias': mlstm_i_bias, 'mlstm_f_bias': mlstm_f_bias, 'mlstm_norm_g': mlstm_norm_g,
            'w_branch': w_branch, 'w_out': w_out, 'norm2_g': norm2_g, 'w_router': w_router,
            'w_gate': w_gate, 'w_up': w_up, 'w_down': w_down, 'final_norm_g': final_norm_g}


def reference(x, norm1_g, w_in, conv_w, gdn_a_log, gdn_dt_bias, gdn_norm_g,
              mlstm_i_bias, mlstm_f_bias, mlstm_norm_g, w_branch, w_out, norm2_g,
              w_router, w_gate, w_up, w_down, final_norm_g):
    pos = jnp.arange(x.shape[1])
    for layer in range(DEPTH):
        h = rms_norm(x, norm1_g[layer])
        x = x + hybrid_mixer(h, pos, w_in[layer], conv_w[layer], gdn_a_log[layer], gdn_dt_bias[layer],
                             gdn_norm_g[layer], mlstm_i_bias[layer], mlstm_f_bias[layer],
                             mlstm_norm_g[layer], w_branch[layer], w_out[layer])
        x = x + expert_choice_moe(rms_norm(x, norm2_g[layer]), w_router[layer], w_gate[layer],
                                  w_up[layer], w_down[layer])
    return rms_norm(x, final_norm_g)
```

```python
import functools

import jax
import jax.numpy as jnp
from jax import lax
from jax.experimental import pallas as pl
from jax.experimental.pallas import tpu as pltpu

F32 = jnp.float32
BF16 = jnp.bfloat16

D_MODEL = 2048
DEPTH = 2
HEAD_DIM = 128
N_HEADS = 6
BRANCH_WIDTH = N_HEADS * HEAD_DIM
CONV_WIDTH = 5
CHUNK = 64
GROUP = 4
ROWS = GROUP * CHUNK
SWA_GROUPS = ((128, 1), (512, 4), (2048, 16))
SWA_HALF = 64
SWA_TQ = 128
ROPE_THETA = 500000.0
ROPE_DIM = HEAD_DIM // 4
N_BRANCH = 3
N_EXPERTS = 16
EXPERT_FF = 1024
EC_CAPACITY_FACTOR = 2
RMS_EPS = 1e-6
NEG_INF = -1e30

OFF_MERGE = 0
OFF_A_QKV = OFF_MERGE + N_BRANCH * D_MODEL
OFF_A_Z = OFF_A_QKV + 3 * BRANCH_WIDTH
OFF_B_QKV = OFF_A_Z + BRANCH_WIDTH
OFF_B_O = OFF_B_QKV + 3 * BRANCH_WIDTH
OFF_C_QKV = OFF_B_O + BRANCH_WIDTH
N_PROJ = OFF_C_QKV + 3 * BRANCH_WIDTH
GATE_LANES = 128
GATE_A = 0
GATE_B = 4 * N_HEADS

VMEM_LIMIT_BYTES = 56 * 1024 * 1024


def _params(*sem):
    return pltpu.CompilerParams(dimension_semantics=sem, vmem_limit_bytes=VMEM_LIMIT_BYTES)


def _dot(a, b):
    return jnp.dot(a.astype(BF16), b.astype(BF16), preferred_element_type=F32)


def _dot_nt(a, b):
    return lax.dot_general(a.astype(BF16), b.astype(BF16), (((1,), (1,)), ((), ())),
                           preferred_element_type=F32)


def _dot_tn(a, b):
    return lax.dot_general(a.astype(BF16), b.astype(BF16), (((0,), (0,)), ((), ())),
                           preferred_element_type=F32)


def _rmsnorm_body(x_ref, g_ref, o_ref):
    x = x_ref[...]
    y = x * lax.rsqrt(jnp.mean(x * x, axis=-1, keepdims=True) + RMS_EPS)
    o_ref[...] = (y * g_ref[...]).astype(o_ref.dtype)


def _rmsnorm(x, g, out_dtype, tm=512):
    m, d = x.shape
    return pl.pallas_call(
        _rmsnorm_body,
        grid=(m // tm,),
        in_specs=[pl.BlockSpec((tm, d), lambda i: (i, 0)), pl.BlockSpec((1, d), lambda i: (0, 0))],
        out_specs=pl.BlockSpec((tm, d), lambda i: (i, 0)),
        out_shape=jax.ShapeDtypeStruct((m, d), out_dtype),
        compiler_params=_params("parallel"),
        name="rmsnorm",
    )(x, g.reshape(1, d))


def _matmul_body(x_ref, w_ref, o_ref):
    o_ref[...] = jnp.dot(x_ref[...], w_ref[...], preferred_element_type=F32).astype(o_ref.dtype)


def _in_proj(h, w, tm=1024, tn=BRANCH_WIDTH):
    m, k = h.shape
    n = w.shape[1]
    return pl.pallas_call(
        _matmul_body,
        grid=(n // tn, m // tm),
        in_specs=[pl.BlockSpec((tm, k), lambda j, i: (i, 0)), pl.BlockSpec((k, tn), lambda j, i: (0, j))],
        out_specs=pl.BlockSpec((tm, tn), lambda j, i: (i, j)),
        out_shape=jax.ShapeDtypeStruct((m, n), F32),
        compiler_params=_params("parallel", "parallel"),
        name="in_proj",
    )(h, w)


def _gate_proj_body(x_ref, w_ref, nat_ref, t_ref):
    g = jnp.dot(x_ref[...], w_ref[...], preferred_element_type=F32)
    nat_ref[...] = g
    t_ref[...] = g.T


def _gate_proj(h, w, tm=512):
    m, k = h.shape
    return pl.pallas_call(
        _gate_proj_body,
        grid=(m // tm,),
        in_specs=[pl.BlockSpec((tm, k), lambda i: (i, 0)), pl.BlockSpec((k, GATE_LANES), lambda i: (0, 0))],
        out_specs=[pl.BlockSpec((tm, GATE_LANES), lambda i: (i, 0)), pl.BlockSpec((GATE_LANES, tm), lambda i: (0, i))],
        out_shape=[jax.ShapeDtypeStruct((m, GATE_LANES), F32), jax.ShapeDtypeStruct((GATE_LANES, m), F32)],
        compiler_params=_params("parallel"),
        name="gate_proj",
    )(h, w)


def _gdn_prep_body(x_ref, w_ref, o_ref):
    x = x_ref[0]
    w = w_ref[...]
    s = x.shape[0]
    row = lax.broadcasted_iota(jnp.int32, x.shape, 0)
    pad = (CONV_WIDTH - 1) // 2
    acc = x * w[pad:pad + 1]
    for tap in range(CONV_WIDTH):
        off = tap - pad
        if off == 0:
            continue
        shifted = pltpu.roll(x, (-off) % s, 0)
        valid = (row + off >= 0) & (row + off < s)
        acc = acc + jnp.where(valid, shifted, 0.0) * w[tap:tap + 1]
    y = acc * jax.nn.sigmoid(acc)
    r = lax.rsqrt(jnp.sum(y * y, axis=-1, keepdims=True) + RMS_EPS)
    j = pl.program_id(1)
    fac = jnp.where(j < N_HEADS, r * (HEAD_DIM ** -0.5), jnp.where(j < 2 * N_HEADS, r, 1.0))
    o_ref[0] = y * fac


def _gdn_prep(proj, conv_w):
    b, s, _ = proj.shape
    nblk = 3 * N_HEADS
    col0 = OFF_A_QKV // HEAD_DIM
    return pl.pallas_call(
        _gdn_prep_body,
        grid=(b, nblk),
        in_specs=[pl.BlockSpec((1, s, HEAD_DIM), lambda i, j: (i, 0, col0 + j)),
                  pl.BlockSpec((CONV_WIDTH, HEAD_DIM), lambda i, j: (0, j))],
        out_specs=pl.BlockSpec((1, s, HEAD_DIM), lambda i, j: (i, 0, j)),
        out_shape=jax.ShapeDtypeStruct((b, s, 3 * BRANCH_WIDTH), F32),
        compiler_params=_params("parallel", "parallel"),
        name="gdn_prep",
    )(proj, conv_w)


def _group_masks(direction):
    ii = lax.broadcasted_iota(jnp.int32, (ROWS, ROWS), 0)
    jj = lax.broadcasted_iota(jnp.int32, (ROWS, ROWS), 1)
    shift = CHUNK.bit_length() - 1
    blk = (ii >> shift) == (jj >> shift)
    rel = (ii - jj) if direction == 0 else (jj - ii)
    return blk, blk & (rel >= 0), blk & (rel > 0), blk & (rel <= 0)


def _softplus(x):
    return jnp.maximum(x, 0.0) + jnp.log1p(jnp.exp(-jnp.abs(x)))


def _log_sigmoid(x):
    return -_softplus(-x)


def _chunk_order(direction):
    return range(GROUP) if direction == 0 else range(GROUP - 1, -1, -1)


def _bidir_specs(b, s, col0):
    ng = s // ROWS

    def specs(grp):
        qkv = [pl.BlockSpec((1, ROWS, BRANCH_WIDTH), lambda i, g, c=c: (i, grp(g), col0 + c)) for c in range(3)]
        return qkv + [pl.BlockSpec((ROWS, GATE_LANES), lambda i, g: (i * ng + grp(g), 0)),
                      pl.BlockSpec((GATE_LANES, ROWS), lambda i, g: (0, i * ng + grp(g)))]

    fwd, bwd = (lambda g: g), (lambda g: ng - 1 - g)
    out = [pl.BlockSpec((1, ROWS, BRANCH_WIDTH), lambda i, g: (i, fwd(g), 0)),
           pl.BlockSpec((1, ROWS, BRANCH_WIDTH), lambda i, g: (i, bwd(g), 0))]
    return specs(fwd) + specs(bwd), out


def _gdn_body(qf, kf, vf, gnf, gtf, qb, kb, vb, gnb, gtb, al_ref, dt_ref, of, ob, state):
    @pl.when(pl.program_id(1) == 0)
    def _():
        state[...] = jnp.zeros_like(state)

    eye = (lax.broadcasted_iota(jnp.int32, (ROWS, ROWS), 0)
           == lax.broadcasted_iota(jnp.int32, (ROWS, ROWS), 1)).astype(F32)
    streams = ((qf, kf, vf, gnf, gtf, of), (qb, kb, vb, gnb, gtb, ob))
    heads = range(N_HEADS)
    sls = [slice(h * HEAD_DIM, (h + 1) * HEAD_DIM) for h in heads]
    for d, (q_ref, k_ref, v_ref, gn_ref, gt_ref, o_ref) in enumerate(streams):
        blk, incl, strict, incl_t = _group_masks(d)
        gn = gn_ref[...]
        gt = gt_ref[...]
        cg = [GATE_A + d * N_HEADS + h for h in heads]
        cb = [c + 2 * N_HEADS for c in cg]
        neg_a = [-jnp.exp(al_ref[d:d + 1, h:h + 1]) for h in heads]
        dtb = [dt_ref[d:d + 1, h:h + 1] for h in heads]
        g_c = [neg_a[h] * _softplus(gn[:, cg[h]:cg[h] + 1] + dtb[h]) for h in heads]
        g_r = [neg_a[h] * _softplus(gt[cg[h]:cg[h] + 1, :] + dtb[h]) for h in heads]
        beta = [jax.nn.sigmoid(gn[:, cb[h]:cb[h] + 1]) for h in heads]
        gc_c = [jnp.sum(jnp.where(incl, g_r[h], 0.0), axis=1, keepdims=True) for h in heads]
        gc_r = [jnp.sum(jnp.where(incl_t, g_c[h], 0.0), axis=0, keepdims=True) for h in heads]
        gtot = [jnp.sum(jnp.where(blk, g_r[h], 0.0), axis=1, keepdims=True) for h in heads]
        decay = [jnp.where(incl, jnp.exp(jnp.where(incl, gc_c[h] - gc_r[h], 0.0)), 0.0) for h in heads]
        q = [q_ref[0, :, sl] for sl in sls]
        k = [k_ref[0, :, sl] for sl in sls]
        v = [v_ref[0, :, sl] for sl in sls]
        gram = [_dot_nt(jnp.concatenate([k[h], q[h]], axis=0), k[h]) for h in heads]
        p = [-jnp.where(strict, beta[h] * decay[h] * gram[h][:ROWS], 0.0) for h in heads]
        t_inv = [eye + p[h] for h in heads]
        p = [_dot(p[h], p[h]) for h in heads]
        for _ in range(4):
            both = [_dot(p[h], jnp.concatenate([t_inv[h], p[h]], axis=1)) for h in heads]
            t_inv = [t_inv[h] + both[h][:, :ROWS] for h in heads]
            p = [both[h][:, ROWS:] for h in heads]
        t_inv = [t_inv[h] + _dot(p[h], t_inv[h]) for h in heads]
        e_c = [jnp.exp(gc_c[h]) for h in heads]
        sol = [_dot(t_inv[h], jnp.concatenate([v[h] * beta[h], k[h] * (beta[h] * e_c[h])], axis=-1)) for h in heads]
        w_v = [sol[h][:, :HEAD_DIM] for h in heads]
        w_k = [sol[h][:, HEAD_DIM:] for h in heads]
        qkd = [gram[h][ROWS:] * decay[h] for h in heads]
        q_dec = [q[h] * e_c[h] for h in heads]
        k_dec = [k[h] * jnp.exp(gtot[h] - gc_c[h]) for h in heads]
        c_dec = [jnp.exp(gtot[h]) for h in heads]
        st = [state[d, h] for h in heads]
        for c in _chunk_order(d):
            r = slice(c * CHUNK, (c + 1) * CHUNK)
            xs = [_dot(jnp.concatenate([w_k[h][r], q_dec[h][r]], axis=0), st[h]) for h in heads]
            u = [w_v[h][r] - xs[h][:CHUNK] for h in heads]
            for h in heads:
                o_ref[0, r, sls[h]] = xs[h][CHUNK:] + _dot(qkd[h][r, r], u[h])
            st = [c_dec[h][c * CHUNK:c * CHUNK + 1] * st[h] + _dot_tn(k_dec[h][r], u[h]) for h in heads]
        for h in heads:
            state[d, h] = st[h]


def _gdn_chunks(qkv, g_nat, g_t, a_log, dt_bias):
    b, s, _ = qkv.shape
    in_specs, out_specs = _bidir_specs(b, s, 0)
    small = pl.BlockSpec((2, N_HEADS), lambda i, g: (0, 0))
    sds = jax.ShapeDtypeStruct((b, s, BRANCH_WIDTH), F32)
    ins = [qkv, qkv, qkv, g_nat, g_t]
    return pl.pallas_call(
        _gdn_body,
        grid=(b, s // ROWS),
        in_specs=in_specs + [small, small],
        out_specs=out_specs,
        out_shape=[sds, sds],
        scratch_shapes=[pltpu.VMEM((2, N_HEADS, HEAD_DIM, HEAD_DIM), F32)],
        compiler_params=_params("parallel", "arbitrary"),
        name="gdn_chunks",
    )(*ins, *ins, a_log, dt_bias)


def _mlstm_body(qf, kf, vf, gnf, gtf, qb, kb, vb, gnb, gtb, ib_ref, fb_ref, of, ob, c_state, n_state, m_state):
    @pl.when(pl.program_id(1) == 0)
    def _():
        c_state[...] = jnp.zeros_like(c_state)
        n_state[...] = jnp.zeros_like(n_state)
        m_state[...] = jnp.zeros_like(m_state)

    ii = lax.broadcasted_iota(jnp.int32, (ROWS, ROWS), 0)
    jj = lax.broadcasted_iota(jnp.int32, (ROWS, ROWS), 1)
    streams = ((qf, kf, vf, gnf, gtf, of), (qb, kb, vb, gnb, gtb, ob))
    for d, (q_ref, k_ref, v_ref, gn_ref, gt_ref, o_ref) in enumerate(streams):
        incl = (jj <= ii) if d == 0 else (jj >= ii)
        incl_t = (ii <= jj) if d == 0 else (ii >= jj)
        gn = gn_ref[...]
        gt = gt_ref[...]
        heads = range(N_HEADS)
        sls = [slice(h * HEAD_DIM, (h + 1) * HEAD_DIM) for h in heads]
        ci = [GATE_B + d * N_HEADS + h for h in heads]
        cf = [c + 2 * N_HEADS for c in ci]
        ib = [ib_ref[d:d + 1, h:h + 1] for h in heads]
        fb = [fb_ref[d:d + 1, h:h + 1] for h in heads]
        q = [q_ref[0, :, sl] for sl in sls]
        v = [v_ref[0, :, sl] for sl in sls]
        k = [k_ref[0, :, sl] * (HEAD_DIM ** -0.5) for sl in sls]
        qk = [_dot_nt(q[h], k[h]) for h in heads]
        cmat = [c_state[d, h] for h in heads]
        nvec = [n_state[d, h] for h in heads]
        m0 = [m_state[d, h][:, :1] for h in heads]
        qc = [_dot(q[h], cmat[h]) for h in heads]
        i_c = [gn[:, ci[h]:ci[h] + 1] + ib[h] for h in heads]
        i_r = [gt[ci[h]:ci[h] + 1, :] + ib[h] for h in heads]
        f_c = [_log_sigmoid(gn[:, cf[h]:cf[h] + 1] + fb[h]) for h in heads]
        f_r = [_log_sigmoid(gt[cf[h]:cf[h] + 1, :] + fb[h]) for h in heads]
        b_c = [jnp.sum(jnp.where(incl, f_r[h], 0.0), axis=1, keepdims=True) for h in heads]
        b_r = [jnp.sum(jnp.where(incl_t, f_c[h], 0.0), axis=0, keepdims=True) for h in heads]
        btot = [jnp.sum(f_c[h], axis=0, keepdims=True) for h in heads]
        log_d = [jnp.where(incl, b_c[h] - b_r[h] + i_r[h], NEG_INF) for h in heads]
        row_max = [jnp.max(log_d[h], axis=1, keepdims=True) for h in heads]
        lend = [btot[h] - b_c[h] + i_c[h] for h in heads]
        m_t = [jnp.maximum(b_c[h] + m0[h], row_max[h]) for h in heads]
        inter = [jnp.exp(b_c[h] + m0[h] - m_t[h]) for h in heads]
        dmat = [jnp.exp(log_d[h] - m_t[h]) * qk[h] for h in heads]
        dv = [_dot(dmat[h], v[h]) for h in heads]
        m_new = [jnp.maximum(btot[h] + m0[h], jnp.max(lend[h], axis=0, keepdims=True)) for h in heads]
        scale = [jnp.exp(btot[h] + m0[h] - m_new[h]) for h in heads]
        kw = [k[h] * jnp.exp(lend[h] - m_new[h]) for h in heads]
        kv = [_dot_tn(kw[h], v[h]) for h in heads]
        for h in heads:
            den = (inter[h] * jnp.sum(q[h] * nvec[h], axis=1, keepdims=True)
                   + jnp.sum(dmat[h], axis=1, keepdims=True))
            o_ref[0, :, sls[h]] = (inter[h] * qc[h] + dv[h]) / jnp.maximum(jnp.abs(den), jnp.exp(-m_t[h]))
            c_state[d, h] = scale[h] * cmat[h] + kv[h]
            n_state[d, h] = scale[h] * nvec[h] + jnp.sum(kw[h], axis=0, keepdims=True)
            m_state[d, h] = jnp.broadcast_to(m_new[h], (1, HEAD_DIM))


def _mlstm_chunks(proj, g_nat, g_t, i_bias, f_bias):
    b, s, _ = proj.shape
    in_specs, out_specs = _bidir_specs(b, s, OFF_B_QKV // BRANCH_WIDTH)
    small = pl.BlockSpec((2, N_HEADS), lambda i, g: (0, 0))
    sds = jax.ShapeDtypeStruct((b, s, BRANCH_WIDTH), F32)
    ins = [proj, proj, proj, g_nat, g_t]
    return pl.pallas_call(
        _mlstm_body,
        grid=(b, s // ROWS),
        in_specs=in_specs + [small, small],
        out_specs=out_specs,
        out_shape=[sds, sds],
        scratch_shapes=[pltpu.VMEM((2, N_HEADS, HEAD_DIM, HEAD_DIM), F32),
                        pltpu.VMEM((2, N_HEADS, 1, HEAD_DIM), F32),
                        pltpu.VMEM((2, N_HEADS, 1, HEAD_DIM), F32)],
        compiler_params=_params("parallel", "arbitrary"),
        name="mlstm_chunks",
    )(*ins, *ins, i_bias, f_bias)


def _rope_tables(s):
    half = ROPE_DIM // 2
    inv_freq = jnp.power(ROPE_THETA, -jnp.arange(half, dtype=F32) * (2.0 / ROPE_DIM))
    ang = jnp.arange(s, dtype=F32)[:, None] * inv_freq[None, :]
    cos, sin = jnp.cos(ang), jnp.sin(ang)
    rest = HEAD_DIM - ROPE_DIM
    cos_t = jnp.concatenate([cos, cos, jnp.ones((s, rest), F32)], axis=-1)
    sin_t = jnp.concatenate([-sin, sin, jnp.zeros((s, rest), F32)], axis=-1)
    return cos_t, sin_t


def _rotary(x, cos_t, sin_t):
    half = ROPE_DIM // 2
    lane = lax.broadcasted_iota(jnp.int32, x.shape, 1)
    swapped = jnp.where(lane < half, pltpu.roll(x, HEAD_DIM - half, 1), pltpu.roll(x, half, 1))
    return x * cos_t + swapped * sin_t


def _swa_body(q_ref, k_ref, v_ref, cos_ref, sin_ref, o_ref, lse_ref, k_rot, v_bf, *, dil, win):
    tb = pl.program_id(2)
    s_len = k_ref.shape[1]

    @pl.when(tb == 0)
    def _():
        k_rot[...] = _rotary(k_ref[0], cos_ref[...], sin_ref[...]).astype(k_rot.dtype)
        v_bf[...] = v_ref[0].astype(v_bf.dtype)

    q0 = pl.multiple_of(tb * SWA_TQ, SWA_TQ)
    start = pl.multiple_of(jnp.clip(q0 - SWA_HALF * dil, 0, s_len - win), SWA_HALF)
    q = _rotary(q_ref[0], cos_ref[pl.ds(q0, SWA_TQ), :], sin_ref[pl.ds(q0, SWA_TQ), :]) * (HEAD_DIM ** -0.5)
    s = _dot_nt(q, k_rot[pl.ds(start, win), :])
    rel = (q0 + lax.broadcasted_iota(jnp.int32, s.shape, 0)) - (start + lax.broadcasted_iota(jnp.int32, s.shape, 1))
    keep = (jnp.abs(rel) <= SWA_HALF * dil) & ((rel & (dil - 1)) == 0)
    s = jnp.where(keep, s, NEG_INF)
    m = jnp.max(s, axis=1, keepdims=True)
    p = jnp.exp(s - m)
    den = jnp.sum(p, axis=1, keepdims=True)
    o_ref[0] = _dot(p, v_bf[pl.ds(start, win), :]) / den
    lse_ref[0] = jnp.broadcast_to(m + jnp.log(den), (SWA_TQ, HEAD_DIM))


def _swa_group(proj, cos_t, sin_t, gi):
    b, s, _ = proj.shape
    dil = SWA_GROUPS[gi][1]
    win = min(s, SWA_TQ + 2 * SWA_HALF * dil)
    qcol = OFF_C_QKV // HEAD_DIM + 2 * gi
    kcol, vcol = qcol + N_HEADS, qcol + 2 * N_HEADS
    table = pl.BlockSpec((s, HEAD_DIM), lambda i, j, tb: (0, 0))
    out_spec = pl.BlockSpec((1, SWA_TQ, HEAD_DIM), lambda i, j, tb: (i, tb, j))
    out_sds = jax.ShapeDtypeStruct((b, s, 2 * HEAD_DIM), F32)
    o, lse = pl.pallas_call(
        functools.partial(_swa_body, dil=dil, win=win),
        grid=(b, 2, s // SWA_TQ),
        in_specs=[pl.BlockSpec((1, SWA_TQ, HEAD_DIM), lambda i, j, tb: (i, tb, qcol + j)),
                  pl.BlockSpec((1, s, HEAD_DIM), lambda i, j, tb: (i, 0, kcol + j)),
                  pl.BlockSpec((1, s, HEAD_DIM), lambda i, j, tb: (i, 0, vcol + j)),
                  table, table],
        out_specs=[out_spec, out_spec],
        out_shape=[out_sds, out_sds],
        scratch_shapes=[pltpu.VMEM((s, HEAD_DIM), BF16), pltpu.VMEM((s, HEAD_DIM), BF16)],
        compiler_params=_params("parallel", "parallel", "arbitrary"),
        name=f"swa_group{gi}",
    )(proj, proj, proj, cos_t, sin_t)
    return o.reshape(b * s, 2 * HEAD_DIM), lse.reshape(b * s, 2 * HEAD_DIM)


def _head_rmsnorm(x, g):
    outs = []
    for h in range(N_HEADS):
        sl = slice(h * HEAD_DIM, (h + 1) * HEAD_DIM)
        seg = x[:, sl]
        outs.append(seg * lax.rsqrt(jnp.mean(seg * seg, axis=-1, keepdims=True) + RMS_EPS) * g[:, sl])
    return jnp.concatenate(outs, axis=-1)


def _merge_body(oa_f, oa_b, ob_f, ob_b, z_ref, op_ref, ga_ref, gb_ref,
                o0, o1, o2, l0, l1, l2, y_ref):
    z = z_ref[...]
    y_a = _head_rmsnorm(oa_f[...] + oa_b[...], ga_ref[...]) * (z * jax.nn.sigmoid(z))
    y_b = _head_rmsnorm(jax.nn.sigmoid(op_ref[...]) * (ob_f[...] + ob_b[...]), gb_ref[...])
    lse = [l0[...], l1[...], l2[...]]
    outs = [o0[...], o1[...], o2[...]]
    m = jnp.maximum(jnp.maximum(lse[0], lse[1]), lse[2])
    e = [jnp.exp(l - m) for l in lse]
    inv = 1.0 / (e[0] + e[1] + e[2])
    y_c = jnp.concatenate([e[g] * inv * outs[g] for g in range(3)], axis=-1)
    y_ref[...] = jnp.concatenate([y_a, y_b, y_c], axis=-1).astype(y_ref.dtype)


def _merge(o_a, o_b, proj2, g_a, g_b, swa_o, swa_lse, tm=256):
    m = proj2.shape[0]
    bw = BRANCH_WIDTH
    row = pl.BlockSpec((tm, bw), lambda i: (i, 0))
    gain = pl.BlockSpec((1, bw), lambda i: (0, 0))
    pair = pl.BlockSpec((tm, 2 * HEAD_DIM), lambda i: (i, 0))
    return pl.pallas_call(
        _merge_body,
        grid=(m // tm,),
        in_specs=[row] * 4
        + [pl.BlockSpec((tm, bw), lambda i: (i, OFF_A_Z // bw)), pl.BlockSpec((tm, bw), lambda i: (i, OFF_B_O // bw)),
           gain, gain] + [pair] * 6,
        out_specs=pl.BlockSpec((tm, N_BRANCH * bw), lambda i: (i, 0)),
        out_shape=jax.ShapeDtypeStruct((m, N_BRANCH * bw), BF16),
        compiler_params=_params("parallel"),
        name="merge_branches",
    )(*[o.reshape(m, bw) for o in (*o_a, *o_b)], proj2, proj2, g_a, g_b, *swa_o, *swa_lse)


def _branch_body(y_ref, w_ref, mp0, mp1, mp2, o_ref):
    acc = None
    for n, mp in enumerate((mp0, mp1, mp2)):
        yn = y_ref[:, n * BRANCH_WIDTH:(n + 1) * BRANCH_WIDTH]
        term = jax.nn.sigmoid(mp[...]) * jnp.dot(yn, w_ref[n], preferred_element_type=F32)
        acc = term if acc is None else acc + term
    o_ref[...] = acc.astype(o_ref.dtype)


def _branch_mix(y, w_branch, proj2, tm=512, tn=512):
    m = y.shape[0]
    nblk = D_MODEL // tn

    def mp_spec(n):
        return pl.BlockSpec((tm, tn), lambda j, i: (i, (OFF_MERGE + n * D_MODEL) // tn + j))

    return pl.pallas_call(
        _branch_body,
        grid=(nblk, m // tm),
        in_specs=[pl.BlockSpec((tm, N_BRANCH * BRANCH_WIDTH), lambda j, i: (i, 0)),
                  pl.BlockSpec((N_BRANCH, BRANCH_WIDTH, tn), lambda j, i: (0, 0, j)),
                  mp_spec(0), mp_spec(1), mp_spec(2)],
        out_specs=pl.BlockSpec((tm, tn), lambda j, i: (i, j)),
        out_shape=jax.ShapeDtypeStruct((m, D_MODEL), BF16),
        compiler_params=_params("parallel", "parallel"),
        name="branch_mix",
    )(y, w_branch, proj2, proj2, proj2)


def _out_proj_body(a_ref, w_ref, x_ref, o_ref):
    o_ref[...] = x_ref[...] + jnp.dot(a_ref[...], w_ref[...], preferred_element_type=F32)


def _out_proj(a, w, x, tm=1024, tn=512):
    m, k = a.shape
    n = w.shape[1]
    return pl.pallas_call(
        _out_proj_body,
        grid=(n // tn, m // tm),
        in_specs=[pl.BlockSpec((tm, k), lambda j, i: (i, 0)), pl.BlockSpec((k, tn), lambda j, i: (0, j)),
                  pl.BlockSpec((tm, tn), lambda j, i: (i, j))],
        out_specs=pl.BlockSpec((tm, tn), lambda j, i: (i, j)),
        out_shape=jax.ShapeDtypeStruct((m, n), F32),
        compiler_params=_params("parallel", "parallel"),
        name="out_proj",
    )(a, w, x)


def _router_body(x_ref, g_ref, w_ref, h_ref, aff_ref):
    x = x_ref[...]
    h = x * lax.rsqrt(jnp.mean(x * x, axis=-1, keepdims=True) + RMS_EPS) * g_ref[...]
    h_ref[...] = h.astype(h_ref.dtype)
    logits = jnp.dot(h, w_ref[...], preferred_element_type=F32, precision=lax.Precision.HIGHEST)
    e = jnp.exp(logits - jnp.max(logits, axis=-1, keepdims=True))
    aff_ref[...] = e / jnp.sum(e, axis=-1, keepdims=True)


def _router(x, g, w_router, tm=512):
    m, d = x.shape
    return pl.pallas_call(
        _router_body,
        grid=(m // tm,),
        in_specs=[pl.BlockSpec((tm, d), lambda i: (i, 0)), pl.BlockSpec((1, d), lambda i: (0, 0)),
                  pl.BlockSpec((d, N_EXPERTS), lambda i: (0, 0))],
        out_specs=[pl.BlockSpec((tm, d), lambda i: (i, 0)), pl.BlockSpec((tm, N_EXPERTS), lambda i: (i, 0))],
        out_shape=[jax.ShapeDtypeStruct((m, d), BF16), jax.ShapeDtypeStruct((m, N_EXPERTS), F32)],
        compiler_params=_params("parallel"),
        name="moe_router",
    )(x, g.reshape(1, d), w_router)


def _route_body(aff_ref, slot_ref, *, cap):
    aff = aff_ref[0]
    e, s = aff.shape
    thr = jnp.zeros((e, 1), jnp.int32)
    for bit in range(30, -1, -1):
        cand = thr | (1 << bit)
        cnt = jnp.sum(jnp.where(aff >= pltpu.bitcast(cand, F32), 1.0, 0.0), axis=1, keepdims=True)
        thr = jnp.where(cnt >= cap, cand, thr)
    thr_f = pltpu.bitcast(thr, F32)
    gt = aff > thr_f
    eq = aff == thr_f
    n_gt = jnp.sum(jnp.where(gt, 1.0, 0.0), axis=1, keepdims=True)
    upper = (lax.broadcasted_iota(jnp.int32, (s, s), 0)
             <= lax.broadcasted_iota(jnp.int32, (s, s), 1)).astype(BF16)
    eq_rank = jnp.dot(jnp.where(eq, 1.0, 0.0).astype(BF16), upper, preferred_element_type=F32)
    sel = gt | (eq & (eq_rank <= cap - n_gt))
    sel_f = jnp.where(sel, 1.0, 0.0)
    pos = jnp.dot(sel_f.astype(BF16), upper, preferred_element_type=F32)
    slot_ref[0] = jnp.where(sel, pos - 1.0, -1.0).astype(jnp.int32)


def _route(aff_t, cap):
    b, e, s = aff_t.shape
    return pl.pallas_call(
        functools.partial(_route_body, cap=cap),
        grid=(b,),
        in_specs=[pl.BlockSpec((1, e, s), lambda i: (i, 0, 0))],
        out_specs=pl.BlockSpec((1, e, s), lambda i: (i, 0, 0)),
        out_shape=jax.ShapeDtypeStruct((b, e, s), jnp.int32),
        compiler_params=_params("parallel"),
        name="moe_route",
    )(aff_t)


def _gather_body(h_ref, slot_ref, aff_ref, xe_ref, gate_ref, *, cap):
    e = pl.program_id(1)
    slot = slot_ref[0, pl.ds(e, 1), :]
    aff = aff_ref[0, pl.ds(e, 1), :]
    s = slot.shape[1]
    onehot = lax.broadcasted_iota(jnp.int32, (cap, s), 0) == slot
    xe = jnp.dot(jnp.where(onehot, 1.0, 0.0).astype(BF16), h_ref[0], preferred_element_type=F32)
    xe_ref[0, 0] = xe.astype(xe_ref.dtype)
    gate = jnp.sum(jnp.where(onehot, aff, 0.0), axis=1, keepdims=True)
    gate_ref[0, 0] = jnp.broadcast_to(gate, (cap, HEAD_DIM))


def _gather(h, slot_t, aff_t, cap):
    b, s, d = h.shape
    e = slot_t.shape[1]
    return pl.pallas_call(
        functools.partial(_gather_body, cap=cap),
        grid=(b, e),
        in_specs=[pl.BlockSpec((1, s, d), lambda i, j: (i, 0, 0)),
                  pl.BlockSpec((1, e, s), lambda i, j: (i, 0, 0)),
                  pl.BlockSpec((1, e, s), lambda i, j: (i, 0, 0))],
        out_specs=[pl.BlockSpec((1, 1, cap, d), lambda i, j: (j, i, 0, 0)),
                   pl.BlockSpec((1, 1, cap, HEAD_DIM), lambda i, j: (j, i, 0, 0))],
        out_shape=[jax.ShapeDtypeStruct((e, b, cap, d), BF16),
                   jax.ShapeDtypeStruct((e, b, cap, HEAD_DIM), F32)],
        compiler_params=_params("parallel", "arbitrary"),
        name="moe_gather",
    )(h, slot_t, aff_t)


def _expert_body(x_ref, wg_ref, wu_ref, wd_ref, gate_ref, y_ref, acc_ref):
    f = pl.program_id(1)
    nb, cap, d = x_ref.shape[1:]
    x = x_ref[0].reshape(nb * cap, d)
    hg = jnp.dot(x, wg_ref[0].astype(BF16), preferred_element_type=F32)
    hu = jnp.dot(x, wu_ref[0].astype(BF16), preferred_element_type=F32)
    hid = (hg * jax.nn.sigmoid(hg) * hu).astype(BF16)
    part = jnp.dot(hid, wd_ref[0].astype(BF16), preferred_element_type=F32)

    @pl.when(f == 0)
    def _():
        acc_ref[...] = part

    @pl.when(f > 0)
    def _():
        acc_ref[...] += part

    @pl.when(f == pl.num_programs(1) - 1)
    def _():
        gate = gate_ref[0].reshape(nb * cap, HEAD_DIM)[:, :1]
        y_ref[:, 0] = (acc_ref[...] * gate).astype(y_ref.dtype).reshape(nb, cap, d)


def _experts(xe, gates, w_gate, w_up, w_down, tf=256):
    e, b, cap, d = xe.shape
    ff = w_gate.shape[2]
    return pl.pallas_call(
        _expert_body,
        grid=(e, ff // tf),
        in_specs=[pl.BlockSpec((1, b, cap, d), lambda i, f: (i, 0, 0, 0)),
                  pl.BlockSpec((1, d, tf), lambda i, f: (i, 0, f)),
                  pl.BlockSpec((1, d, tf), lambda i, f: (i, 0, f)),
                  pl.BlockSpec((1, tf, d), lambda i, f: (i, f, 0)),
                  pl.BlockSpec((1, b, cap, HEAD_DIM), lambda i, f: (i, 0, 0, 0))],
        out_specs=pl.BlockSpec((b, 1, cap, d), lambda i, f: (0, i, 0, 0)),
        out_shape=jax.ShapeDtypeStruct((b, e, cap, d), BF16),
        scratch_shapes=[pltpu.VMEM((b * cap, d), F32)],
        compiler_params=_params("parallel", "arbitrary"),
        name="moe_experts",
    )(xe, w_gate, w_up, w_down, gates)


def _combine_body(x_ref, slot_ref, ye_ref, g_ref, o_ref, *, final_norm):
    ne, cap, d = ye_ref.shape[1:]
    slot = slot_ref[0]
    lane = lax.broadcasted_iota(jnp.int32, (slot.shape[0], cap), 1)
    onehot = jnp.concatenate(
        [jnp.where(slot[:, e:e + 1] == lane, 1.0, 0.0).astype(BF16) for e in range(ne)], axis=-1)
    out = x_ref[0] + jnp.dot(onehot, ye_ref[0].reshape(ne * cap, d), preferred_element_type=F32)
    if final_norm:
        out = out * lax.rsqrt(jnp.mean(out * out, axis=-1, keepdims=True) + RMS_EPS) * g_ref[...]
    o_ref[0] = out


def _combine(x, slot_nat, ye, final_g, final_norm, ts=256):
    b, s, d = x.shape
    _, e, cap, _ = ye.shape
    return pl.pallas_call(
        functools.partial(_combine_body, final_norm=final_norm),
        grid=(b, s // ts),
        in_specs=[pl.BlockSpec((1, ts, d), lambda i, j: (i, j, 0)),
                  pl.BlockSpec((1, ts, e), lambda i, j: (i, j, 0)),
                  pl.BlockSpec((1, e, cap, d), lambda i, j: (i, 0, 0, 0)),
                  pl.BlockSpec((1, d), lambda i, j: (0, 0))],
        out_specs=pl.BlockSpec((1, ts, d), lambda i, j: (i, j, 0)),
        out_shape=jax.ShapeDtypeStruct((b, s, d), F32),
        compiler_params=_params("parallel", "arbitrary"),
        name="moe_combine",
    )(x, slot_nat, ye, final_g.reshape(1, d))


def _split_w_in(w):
    bw = BRANCH_WIDTH
    sizes = (3 * bw, bw, 4 * N_HEADS, 3 * bw, bw, 4 * N_HEADS, 3 * bw, N_BRANCH * D_MODEL)
    offs = [0]
    for sz in sizes:
        offs.append(offs[-1] + sz)
    a_qkv, a_z, a_gate, b_qkv, b_o, b_gate, c_qkv, merge = [w[:, offs[i]:offs[i + 1]] for i in range(8)]
    main = jnp.concatenate([merge, a_qkv, a_z, b_qkv, b_o, c_qkv], axis=1).astype(BF16)
    pad = jnp.zeros((w.shape[0], GATE_LANES - 8 * N_HEADS), w.dtype)
    return main, jnp.concatenate([a_gate, b_gate, pad], axis=1).astype(BF16)


def _mixer(x2, b, s, norm_g, w_in, conv_w, a_log, dt_bias, gdn_g, i_bias, f_bias, mlstm_g, w_branch, w_out,
           rope):
    h = _rmsnorm(x2, norm_g, BF16)
    w_main, w_gates = _split_w_in(w_in)
    proj2 = _in_proj(h, w_main)
    g_nat, g_t = _gate_proj(h, w_gates)
    proj = proj2.reshape(b, s, N_PROJ)
    o_a = _gdn_chunks(_gdn_prep(proj, conv_w), g_nat, g_t, a_log, dt_bias)
    o_b = _mlstm_chunks(proj, g_nat, g_t, i_bias, f_bias)
    swa = [_swa_group(proj, rope[0], rope[1], gi) for gi in range(len(SWA_GROUPS))]
    y = _merge(o_a, o_b, proj2, jnp.tile(gdn_g, N_HEADS).reshape(1, BRANCH_WIDTH),
               mlstm_g.reshape(1, BRANCH_WIDTH), [o for o, _ in swa], [l for _, l in swa])
    mixed = _branch_mix(y, w_branch.astype(BF16), proj2)
    return _out_proj(mixed, w_out.astype(BF16), x2)


def _moe(x2, b, s, norm_g, w_router, w_gate, w_up, w_down, final_g, final_norm):
    cap = EC_CAPACITY_FACTOR * s // N_EXPERTS
    h, aff = _router(x2, norm_g, w_router)
    aff_t = jnp.swapaxes(aff.reshape(b, s, N_EXPERTS), 1, 2)
    slot_t = _route(aff_t, cap)
    xe, gates = _gather(h.reshape(b, s, D_MODEL), slot_t, aff_t, cap)
    ye = _experts(xe, gates, w_gate, w_up, w_down)
    out = _combine(x2.reshape(b, s, D_MODEL), jnp.swapaxes(slot_t, 1, 2), ye, final_g, final_norm)
    return out.reshape(b * s, D_MODEL)


def kernel(x, norm1_g, w_in, conv_w, gdn_a_log, gdn_dt_bias, gdn_norm_g, mlstm_i_bias, mlstm_f_bias, mlstm_norm_g, w_branch, w_out, norm2_g, w_router, w_gate, w_up, w_down, final_norm_g):
    b, s, d = x.shape
    rope = _rope_tables(s)
    x2 = x.reshape(b * s, d)
    for layer in range(DEPTH):
        x2 = _mixer(x2, b, s, norm1_g[layer], w_in[layer], conv_w[layer], gdn_a_log[layer], gdn_dt_bias[layer],
                    gdn_norm_g[layer], mlstm_i_bias[layer], mlstm_f_bias[layer], mlstm_norm_g[layer],
                    w_branch[layer], w_out[layer], rope)
        x2 = _moe(x2, b, s, norm2_g[layer], w_router[layer], w_gate[layer], w_up[layer], w_down[layer],
                  final_norm_g, layer == DEPTH - 1)
    return x2.reshape(b, s, d)
```

```python
import functools

import jax
import jax.numpy as jnp
from jax import lax
from jax.experimental import pallas as pl
from jax.experimental.pallas import tpu as pltpu

F32 = jnp.float32
BF16 = jnp.bfloat16

D_MODEL = 2048
DEPTH = 2
HEAD_DIM = 128
N_HEADS = 6
BRANCH_WIDTH = N_HEADS * HEAD_DIM
CONV_WIDTH = 5
CHUNK = 64
GROUP = 4
ROWS = GROUP * CHUNK
SWA_GROUPS = ((128, 1), (512, 4), (2048, 16))
SWA_HALF = 64
SWA_TQ = 128
ROPE_THETA = 500000.0
ROPE_DIM = HEAD_DIM // 4
N_BRANCH = 3
N_EXPERTS = 16
EXPERT_FF = 1024
EC_CAPACITY_FACTOR = 2
RMS_EPS = 1e-6
NEG_INF = -1e30

OFF_MERGE = 0
OFF_A_QKV = OFF_MERGE + N_BRANCH * D_MODEL
OFF_A_Z = OFF_A_QKV + 3 * BRANCH_WIDTH
OFF_B_QKV = OFF_A_Z + BRANCH_WIDTH
OFF_B_O = OFF_B_QKV + 3 * BRANCH_WIDTH
OFF_C_QKV = OFF_B_O + BRANCH_WIDTH
N_PROJ = OFF_C_QKV + 3 * BRANCH_WIDTH
GATE_LANES = 128
GATE_A = 0
GATE_B = 4 * N_HEADS

VMEM_LIMIT_BYTES = 56 * 1024 * 1024


def _params(*sem):
    return pltpu.CompilerParams(dimension_semantics=sem, vmem_limit_bytes=VMEM_LIMIT_BYTES)


def _dot(a, b):
    return jnp.dot(a.astype(BF16), b.astype(BF16), preferred_element_type=F32)


def _dot_nt(a, b):
    return lax.dot_general(a.astype(BF16), b.astype(BF16), (((1,), (1,)), ((), ())),
                           preferred_element_type=F32)


def _dot_tn(a, b):
    return lax.dot_general(a.astype(BF16), b.astype(BF16), (((0,), (0,)), ((), ())),
                           preferred_element_type=F32)


def _rmsnorm_body(x_ref, g_ref, o_ref):
    x = x_ref[...]
    y = x * lax.rsqrt(jnp.mean(x * x, axis=-1, keepdims=True) + RMS_EPS)
    o_ref[...] = (y * g_ref[...]).astype(o_ref.dtype)


def _rmsnorm(x, g, out_dtype, tm=512):
    m, d = x.shape
    return pl.pallas_call(
        _rmsnorm_body,
        grid=(m // tm,),
        in_specs=[pl.BlockSpec((tm, d), lambda i: (i, 0)), pl.BlockSpec((1, d), lambda i: (0, 0))],
        out_specs=pl.BlockSpec((tm, d), lambda i: (i, 0)),
        out_shape=jax.ShapeDtypeStruct((m, d), out_dtype),
        compiler_params=_params("parallel"),
        name="rmsnorm",
    )(x, g.reshape(1, d))


def _matmul_body(x_ref, w_ref, o_ref):
    o_ref[...] = jnp.dot(x_ref[...], w_ref[...], preferred_element_type=F32).astype(o_ref.dtype)


def _in_proj(h, w, tm=1024, tn=BRANCH_WIDTH):
    m, k = h.shape
    n = w.shape[1]
    return pl.pallas_call(
        _matmul_body,
        grid=(n // tn, m // tm),
        in_specs=[pl.BlockSpec((tm, k), lambda j, i: (i, 0)), pl.BlockSpec((k, tn), lambda j, i: (0, j))],
        out_specs=pl.BlockSpec((tm, tn), lambda j, i: (i, j)),
        out_shape=jax.ShapeDtypeStruct((m, n), F32),
        compiler_params=_params("parallel", "parallel"),
        name="in_proj",
    )(h, w)


def _gate_proj_body(x_ref, w_ref, nat_ref, t_ref):
    g = jnp.dot(x_ref[...], w_ref[...], preferred_element_type=F32)
    nat_ref[...] = g
    t_ref[...] = g.T


def _gate_proj(h, w, tm=512):
    m, k = h.shape
    return pl.pallas_call(
        _gate_proj_body,
        grid=(m // tm,),
        in_specs=[pl.BlockSpec((tm, k), lambda i: (i, 0)), pl.BlockSpec((k, GATE_LANES), lambda i: (0, 0))],
        out_specs=[pl.BlockSpec((tm, GATE_LANES), lambda i: (i, 0)), pl.BlockSpec((GATE_LANES, tm), lambda i: (0, i))],
        out_shape=[jax.ShapeDtypeStruct((m, GATE_LANES), F32), jax.ShapeDtypeStruct((GATE_LANES, m), F32)],
        compiler_params=_params("parallel"),
        name="gate_proj",
    )(h, w)


def _gdn_prep_body(x_ref, w_ref, o_ref):
    x = x_ref[0]
    w = w_ref[...]
    s = x.shape[0]
    row = lax.broadcasted_iota(jnp.int32, x.shape, 0)
    pad = (CONV_WIDTH - 1) // 2
    acc = x * w[pad:pad + 1]
    for tap in range(CONV_WIDTH):
        off = tap - pad
        if off == 0:
            continue
        shifted = pltpu.roll(x, (-off) % s, 0)
        valid = (row + off >= 0) & (row + off < s)
        acc = acc + jnp.where(valid, shifted, 0.0) * w[tap:tap + 1]
    y = acc * jax.nn.sigmoid(acc)
    r = lax.rsqrt(jnp.sum(y * y, axis=-1, keepdims=True) + RMS_EPS)
    j = pl.program_id(1)
    fac = jnp.where(j < N_HEADS, r * (HEAD_DIM ** -0.5), jnp.where(j < 2 * N_HEADS, r, 1.0))
    o_ref[0] = y * fac


def _gdn_prep(proj, conv_w):
    b, s, _ = proj.shape
    nblk = 3 * N_HEADS
    col0 = OFF_A_QKV // HEAD_DIM
    return pl.pallas_call(
        _gdn_prep_body,
        grid=(b, nblk),
        in_specs=[pl.BlockSpec((1, s, HEAD_DIM), lambda i, j: (i, 0, col0 + j)),
                  pl.BlockSpec((CONV_WIDTH, HEAD_DIM), lambda i, j: (0, j))],
        out_specs=pl.BlockSpec((1, s, HEAD_DIM), lambda i, j: (i, 0, j)),
        out_shape=jax.ShapeDtypeStruct((b, s, 3 * BRANCH_WIDTH), F32),
        compiler_params=_params("parallel", "parallel"),
        name="gdn_prep",
    )(proj, conv_w)


def _group_masks(direction):
    ii = lax.broadcasted_iota(jnp.int32, (ROWS, ROWS), 0)
    jj = lax.broadcasted_iota(jnp.int32, (ROWS, ROWS), 1)
    shift = CHUNK.bit_length() - 1
    blk = (ii >> shift) == (jj >> shift)
    rel = (ii - jj) if direction == 0 else (jj - ii)
    return blk, blk & (rel >= 0), blk & (rel > 0), blk & (rel <= 0)


def _softplus(x):
    return jnp.maximum(x, 0.0) + jnp.log1p(jnp.exp(-jnp.abs(x)))


def _log_sigmoid(x):
    return -_softplus(-x)


def _chunk_order(direction):
    return range(GROUP) if direction == 0 else range(GROUP - 1, -1, -1)


def _bidir_specs(b, s, col0):
    ng = s // ROWS

    def specs(grp):
        qkv = [pl.BlockSpec((1, ROWS, BRANCH_WIDTH), lambda i, g, c=c: (i, grp(g), col0 + c)) for c in range(3)]
        return qkv + [pl.BlockSpec((ROWS, GATE_LANES), lambda i, g: (i * ng + grp(g), 0)),
                      pl.BlockSpec((GATE_LANES, ROWS), lambda i, g: (0, i * ng + grp(g)))]

    fwd, bwd = (lambda g: g), (lambda g: ng - 1 - g)
    out = [pl.BlockSpec((1, ROWS, BRANCH_WIDTH), lambda i, g: (i, fwd(g), 0)),
           pl.BlockSpec((1, ROWS, BRANCH_WIDTH), lambda i, g: (i, bwd(g), 0))]
    return specs(fwd) + specs(bwd), out


def _gdn_body(qf, kf, vf, gnf, gtf, qb, kb, vb, gnb, gtb, al_ref, dt_ref, of, ob, state):
    @pl.when(pl.program_id(1) == 0)
    def _():
        state[...] = jnp.zeros_like(state)

    eye = (lax.broadcasted_iota(jnp.int32, (ROWS, ROWS), 0)
           == lax.broadcasted_iota(jnp.int32, (ROWS, ROWS), 1)).astype(F32)
    streams = ((qf, kf, vf, gnf, gtf, of), (qb, kb, vb, gnb, gtb, ob))
    heads = range(N_HEADS)
    sls = [slice(h * HEAD_DIM, (h + 1) * HEAD_DIM) for h in heads]
    for d, (q_ref, k_ref, v_ref, gn_ref, gt_ref, o_ref) in enumerate(streams):
        blk, incl, strict, incl_t = _group_masks(d)
        gn = gn_ref[...]
        gt = gt_ref[...]
        cg = [GATE_A + d * N_HEADS + h for h in heads]
        cb = [c + 2 * N_HEADS for c in cg]
        neg_a = [-jnp.exp(al_ref[d:d + 1, h:h + 1]) for h in heads]
        dtb = [dt_ref[d:d + 1, h:h + 1] for h in heads]
        g_c = [neg_a[h] * _softplus(gn[:, cg[h]:cg[h] + 1] + dtb[h]) for h in heads]
        g_r = [neg_a[h] * _softplus(gt[cg[h]:cg[h] + 1, :] + dtb[h]) for h in heads]
        beta = [jax.nn.sigmoid(gn[:, cb[h]:cb[h] + 1]) for h in heads]
        gc_c = [jnp.sum(jnp.where(incl, g_r[h], 0.0), axis=1, keepdims=True) for h in heads]
        gc_r = [jnp.sum(jnp.where(incl_t, g_c[h], 0.0), axis=0, keepdims=True) for h in heads]
        gtot = [jnp.sum(jnp.where(blk, g_r[h], 0.0), axis=1, keepdims=True) for h in heads]
        decay = [jnp.where(incl, jnp.exp(jnp.where(incl, gc_c[h] - gc_r[h], 0.0)), 0.0) for h in heads]
        q = [q_ref[0, :, sl] for sl in sls]
        k = [k_ref[0, :, sl] for sl in sls]
        v = [v_ref[0, :, sl] for sl in sls]
        gram = [_dot_nt(jnp.concatenate([k[h], q[h]], axis=0), k[h]) for h in heads]
        p = [-jnp.where(strict, beta[h] * decay[h] * gram[h][:ROWS], 0.0) for h in heads]
        t_inv = [eye + p[h] for h in heads]
        p = [_dot(p[h], p[h]) for h in heads]
        for _ in range(4):
            both = [_dot(p[h], jnp.concatenate([t_inv[h], p[h]], axis=1)) for h in heads]
            t_inv = [t_inv[h] + both[h][:, :ROWS] for h in heads]
            p = [both[h][:, ROWS:] for h in heads]
        t_inv = [t_inv[h] + _dot(p[h], t_inv[h]) for h in heads]
        e_c = [jnp.exp(gc_c[h]) for h in heads]
        sol = [_dot(t_inv[h], jnp.concatenate([v[h] * beta[h], k[h] * (beta[h] * e_c[h])], axis=-1)) for h in heads]
        w_v = [sol[h][:, :HEAD_DIM] for h in heads]
        w_k = [sol[h][:, HEAD_DIM:] for h in heads]
        qkd = [gram[h][ROWS:] * decay[h] for h in heads]
        q_dec = [q[h] * e_c[h] for h in heads]
        k_dec = [k[h] * jnp.exp(gtot[h] - gc_c[h]) for h in heads]
        c_dec = [jnp.exp(gtot[h]) for h in heads]
        st = [state[d, h] for h in heads]
        for c in _chunk_order(d):
            r = slice(c * CHUNK, (c + 1) * CHUNK)
            xs = [_dot(jnp.concatenate([w_k[h][r], q_dec[h][r]], axis=0), st[h]) for h in heads]
            u = [w_v[h][r] - xs[h][:CHUNK] for h in heads]
            for h in heads:
                o_ref[0, r, sls[h]] = xs[h][CHUNK:] + _dot(qkd[h][r, r], u[h])
            st = [c_dec[h][c * CHUNK:c * CHUNK + 1] * st[h] + _dot_tn(k_dec[h][r], u[h]) for h in heads]
        for h in heads:
            state[d, h] = st[h]


def _gdn_chunks(qkv, g_nat, g_t, a_log, dt_bias):
    b, s, _ = qkv.shape
    in_specs, out_specs = _bidir_specs(b, s, 0)
    small = pl.BlockSpec((2, N_HEADS), lambda i, g: (0, 0))
    sds = jax.ShapeDtypeStruct((b, s, BRANCH_WIDTH), F32)
    ins = [qkv, qkv, qkv, g_nat, g_t]
    return pl.pallas_call(
        _gdn_body,
        grid=(b, s // ROWS),
        in_specs=in_specs + [small, small],
        out_specs=out_specs,
        out_shape=[sds, sds],
        scratch_shapes=[pltpu.VMEM((2, N_HEADS, HEAD_DIM, HEAD_DIM), F32)],
        compiler_params=_params("parallel", "arbitrary"),
        name="gdn_chunks",
    )(*ins, *ins, a_log, dt_bias)


def _mlstm_body(qf, kf, vf, gnf, gtf, qb, kb, vb, gnb, gtb, ib_ref, fb_ref, of, ob, c_state, n_state, m_state):
    @pl.when(pl.program_id(1) == 0)
    def _():
        c_state[...] = jnp.zeros_like(c_state)
        n_state[...] = jnp.zeros_like(n_state)
        m_state[...] = jnp.zeros_like(m_state)

    ii = lax.broadcasted_iota(jnp.int32, (ROWS, ROWS), 0)
    jj = lax.broadcasted_iota(jnp.int32, (ROWS, ROWS), 1)
    streams = ((qf, kf, vf, gnf, gtf, of), (qb, kb, vb, gnb, gtb, ob))
    for d, (q_ref, k_ref, v_ref, gn_ref, gt_ref, o_ref) in enumerate(streams):
        incl = (jj <= ii) if d == 0 else (jj >= ii)
        incl_t = (ii <= jj) if d == 0 else (ii >= jj)
        gn = gn_ref[...]
        gt = gt_ref[...]
        heads = range(N_HEADS)
        sls = [slice(h * HEAD_DIM, (h + 1) * HEAD_DIM) for h in heads]
        ci = [GATE_B + d * N_HEADS + h for h in heads]
        cf = [c + 2 * N_HEADS for c in ci]
        ib = [ib_ref[d:d + 1, h:h + 1] for h in heads]
        fb = [fb_ref[d:d + 1, h:h + 1] for h in heads]
        q = [q_ref[0, :, sl] for sl in sls]
        v = [v_ref[0, :, sl] for sl in sls]
        k = [k_ref[0, :, sl] * (HEAD_DIM ** -0.5) for sl in sls]
        qk = [_dot_nt(q[h], k[h]) for h in heads]
        cmat = [c_state[d, h] for h in heads]
        nvec = [n_state[d, h] for h in heads]
        m0 = [m_state[d, h][:, :1] for h in heads]
        qc = [_dot(q[h], cmat[h]) for h in heads]
        i_c = [gn[:, ci[h]:ci[h] + 1] + ib[h] for h in heads]
        i_r = [gt[ci[h]:ci[h] + 1, :] + ib[h] for h in heads]
        f_c = [_log_sigmoid(gn[:, cf[h]:cf[h] + 1] + fb[h]) for h in heads]
        f_r = [_log_sigmoid(gt[cf[h]:cf[h] + 1, :] + fb[h]) for h in heads]
        b_c = [jnp.sum(jnp.where(incl, f_r[h], 0.0), axis=1, keepdims=True) for h in heads]
        b_r = [jnp.sum(jnp.where(incl_t, f_c[h], 0.0), axis=0, keepdims=True) for h in heads]
        btot = [jnp.sum(f_c[h], axis=0, keepdims=True) for h in heads]
        log_d = [jnp.where(incl, b_c[h] - b_r[h] + i_r[h], NEG_INF) for h in heads]
        row_max = [jnp.max(log_d[h], axis=1, keepdims=True) for h in heads]
        lend = [btot[h] - b_c[h] + i_c[h] for h in heads]
        m_t = [jnp.maximum(b_c[h] + m0[h], row_max[h]) for h in heads]
        inter = [jnp.exp(b_c[h] + m0[h] - m_t[h]) for h in heads]
        dmat = [jnp.exp(log_d[h] - m_t[h]) * qk[h] for h in heads]
        dv = [_dot(dmat[h], v[h]) for h in heads]
        m_new = [jnp.maximum(btot[h] + m0[h], jnp.max(lend[h], axis=0, keepdims=True)) for h in heads]
        scale = [jnp.exp(btot[h] + m0[h] - m_new[h]) for h in heads]
        kw = [k[h] * jnp.exp(lend[h] - m_new[h]) for h in heads]
        kv = [_dot_tn(kw[h], v[h]) for h in heads]
        for h in heads:
            den = (inter[h] * jnp.sum(q[h] * nvec[h], axis=1, keepdims=True)
                   + jnp.sum(dmat[h], axis=1, keepdims=True))
            o_ref[0, :, sls[h]] = (inter[h] * qc[h] + dv[h]) / jnp.maximum(jnp.abs(den), jnp.exp(-m_t[h]))
            c_state[d, h] = scale[h] * cmat[h] + kv[h]
            n_state[d, h] = scale[h] * nvec[h] + jnp.sum(kw[h], axis=0, keepdims=True)
            m_state[d, h] = jnp.broadcast_to(m_new[h], (1, HEAD_DIM))


def _mlstm_chunks(proj, g_nat, g_t, i_bias, f_bias):
    b, s, _ = proj.shape
    in_specs, out_specs = _bidir_specs(b, s, OFF_B_QKV // BRANCH_WIDTH)
    small = pl.BlockSpec((2, N_HEADS), lambda i, g: (0, 0))
    sds = jax.ShapeDtypeStruct((b, s, BRANCH_WIDTH), F32)
    ins = [proj, proj, proj, g_nat, g_t]
    return pl.pallas_call(
        _mlstm_body,
        grid=(b, s // ROWS),
        in_specs=in_specs + [small, small],
        out_specs=out_specs,
        out_shape=[sds, sds],
        scratch_shapes=[pltpu.VMEM((2, N_HEADS, HEAD_DIM, HEAD_DIM), F32),
                        pltpu.VMEM((2, N_HEADS, 1, HEAD_DIM), F32),
                        pltpu.VMEM((2, N_HEADS, 1, HEAD_DIM), F32)],
        compiler_params=_params("parallel", "arbitrary"),
        name="mlstm_chunks",
    )(*ins, *ins, i_bias, f_bias)


def _rope_tables(s):
    half = ROPE_DIM // 2
    inv_freq = jnp.power(ROPE_THETA, -jnp.arange(half, dtype=F32) * (2.0 / ROPE_DIM))
    ang = jnp.arange(s, dtype=F32)[:, None] * inv_freq[None, :]
    cos, sin = jnp.cos(ang), jnp.sin(ang)
    rest = HEAD_DIM - ROPE_DIM
    cos_t = jnp.concatenate([cos, cos, jnp.ones((s, rest), F32)], axis=-1)
    sin_t = jnp.concatenate([-sin, sin, jnp.zeros((s, rest), F32)], axis=-1)
    return cos_t, sin_t


def _rotary(x, cos_t, sin_t):
    half = ROPE_DIM // 2
    lane = lax.broadcasted_iota(jnp.int32, x.shape, 1)
    swapped = jnp.where(lane < half, pltpu.roll(x, HEAD_DIM - half, 1), pltpu.roll(x, half, 1))
    return x * cos_t + swapped * sin_t


def _swa_body(q_ref, k_ref, v_ref, cos_ref, sin_ref, o_ref, lse_ref, q_rot, k_rot, *, dil):
    s_len = q_ref.shape[1]
    t = s_len // dil
    win = min(t, SWA_TQ + 2 * SWA_HALF)
    cos_t, sin_t = cos_ref[...], sin_ref[...]
    q_rot[...] = _rotary(q_ref[0], cos_t, sin_t) * (HEAD_DIM ** -0.5)
    k_rot[...] = _rotary(k_ref[0], cos_t, sin_t)

    def rows(first, count):
        return pl.ds(first, count, stride=dil) if dil > 1 else pl.ds(first, count)

    blocks = [(r, tb) for r in range(dil) for tb in range(t // SWA_TQ)]
    batch = 4
    for b0 in range(0, len(blocks), batch):
        grp = blocks[b0:b0 + batch]
        starts = [min(max(tb * SWA_TQ - SWA_HALF, 0), t - win) for _, tb in grp]
        q_rows = [rows(r + tb * SWA_TQ * dil, SWA_TQ) for r, tb in grp]
        kv_rows = [rows(r + st * dil, win) for (r, _), st in zip(grp, starts)]
        sc = [_dot_nt(q_rot[qr, :], k_rot[kr, :]) for qr, kr in zip(q_rows, kv_rows)]
        ms, ps, dens = [], [], []
        for (_, tb), st, s in zip(grp, starts, sc):
            rel = (tb * SWA_TQ - st) + (lax.broadcasted_iota(jnp.int32, s.shape, 0)
                                        - lax.broadcasted_iota(jnp.int32, s.shape, 1))
            s = jnp.where(jnp.abs(rel) <= SWA_HALF, s, NEG_INF)
            m = jnp.max(s, axis=1, keepdims=True)
            p = jnp.exp(s - m)
            ms.append(m)
            ps.append(p)
            dens.append(jnp.sum(p, axis=1, keepdims=True))
        pv = [_dot(p, v_ref[0, kr, :]) for p, kr in zip(ps, kv_rows)]
        for qr, o, m, den in zip(q_rows, pv, ms, dens):
            o_ref[0, qr, :] = o / den
            lse_ref[0, qr, :] = jnp.broadcast_to(m + jnp.log(den), (SWA_TQ, HEAD_DIM))


def _swa_group(proj, cos_t, sin_t, gi):
    b, s, _ = proj.shape
    dil = SWA_GROUPS[gi][1]
    qcol = OFF_C_QKV // HEAD_DIM + 2 * gi
    kcol, vcol = qcol + N_HEADS, qcol + 2 * N_HEADS
    table = pl.BlockSpec((s, HEAD_DIM), lambda i, j: (0, 0))
    out_spec = pl.BlockSpec((1, s, HEAD_DIM), lambda i, j: (i, 0, j))
    out_sds = jax.ShapeDtypeStruct((b, s, 2 * HEAD_DIM), F32)
    o, lse = pl.pallas_call(
        functools.partial(_swa_body, dil=dil),
        grid=(b, 2),
        in_specs=[pl.BlockSpec((1, s, HEAD_DIM), lambda i, j: (i, 0, qcol + j)),
                  pl.BlockSpec((1, s, HEAD_DIM), lambda i, j: (i, 0, kcol + j)),
                  pl.BlockSpec((1, s, HEAD_DIM), lambda i, j: (i, 0, vcol + j)),
                  table, table],
        out_specs=[out_spec, out_spec],
        out_shape=[out_sds, out_sds],
        scratch_shapes=[pltpu.VMEM((s, HEAD_DIM), F32), pltpu.VMEM((s, HEAD_DIM), F32)],
        compiler_params=_params("parallel", "parallel"),
        name=f"swa_group{gi}",
    )(proj, proj, proj, cos_t, sin_t)
    return o.reshape(b * s, 2 * HEAD_DIM), lse.reshape(b * s, 2 * HEAD_DIM)


def _head_rmsnorm(x, g):
    outs = []
    for h in range(N_HEADS):
        sl = slice(h * HEAD_DIM, (h + 1) * HEAD_DIM)
        seg = x[:, sl]
        outs.append(seg * lax.rsqrt(jnp.mean(seg * seg, axis=-1, keepdims=True) + RMS_EPS) * g[:, sl])
    return jnp.concatenate(outs, axis=-1)


def _merge_body(oa_f, oa_b, ob_f, ob_b, z_ref, op_ref, ga_ref, gb_ref,
                o0, o1, o2, l0, l1, l2, y_ref):
    z = z_ref[...]
    y_a = _head_rmsnorm(oa_f[...] + oa_b[...], ga_ref[...]) * (z * jax.nn.sigmoid(z))
    y_b = _head_rmsnorm(jax.nn.sigmoid(op_ref[...]) * (ob_f[...] + ob_b[...]), gb_ref[...])
    lse = [l0[...], l1[...], l2[...]]
    outs = [o0[...], o1[...], o2[...]]
    m = jnp.maximum(jnp.maximum(lse[0], lse[1]), lse[2])
    e = [jnp.exp(l - m) for l in lse]
    inv = 1.0 / (e[0] + e[1] + e[2])
    y_c = jnp.concatenate([e[g] * inv * outs[g] for g in range(3)], axis=-1)
    y_ref[...] = jnp.concatenate([y_a, y_b, y_c], axis=-1).astype(y_ref.dtype)


def _merge(o_a, o_b, proj2, g_a, g_b, swa_o, swa_lse, tm=256):
    m = proj2.shape[0]
    bw = BRANCH_WIDTH
    row = pl.BlockSpec((tm, bw), lambda i: (i, 0))
    gain = pl.BlockSpec((1, bw), lambda i: (0, 0))
    pair = pl.BlockSpec((tm, 2 * HEAD_DIM), lambda i: (i, 0))
    return pl.pallas_call(
        _merge_body,
        grid=(m // tm,),
        in_specs=[row] * 4
        + [pl.BlockSpec((tm, bw), lambda i: (i, OFF_A_Z // bw)), pl.BlockSpec((tm, bw), lambda i: (i, OFF_B_O // bw)),
           gain, gain] + [pair] * 6,
        out_specs=pl.BlockSpec((tm, N_BRANCH * bw), lambda i: (i, 0)),
        out_shape=jax.ShapeDtypeStruct((m, N_BRANCH * bw), BF16),
        compiler_params=_params("parallel"),
        name="merge_branches",
    )(*[o.reshape(m, bw) for o in (*o_a, *o_b)], proj2, proj2, g_a, g_b, *swa_o, *swa_lse)


def _branch_body(y_ref, w_ref, mp0, mp1, mp2, o_ref):
    acc = None
    for n, mp in enumerate((mp0, mp1, mp2)):
        yn = y_ref[:, n * BRANCH_WIDTH:(n + 1) * BRANCH_WIDTH]
        term = jax.nn.sigmoid(mp[...]) * jnp.dot(yn, w_ref[n], preferred_element_type=F32)
        acc = term if acc is None else acc + term
    o_ref[...] = acc.astype(o_ref.dtype)


def _branch_mix(y, w_branch, proj2, tm=512, tn=1024):
    m = y.shape[0]
    nblk = D_MODEL // tn

    def mp_spec(n):
        return pl.BlockSpec((tm, tn), lambda j, i: (i, (OFF_MERGE + n * D_MODEL) // tn + j))

    return pl.pallas_call(
        _branch_body,
        grid=(nblk, m // tm),
        in_specs=[pl.BlockSpec((tm, N_BRANCH * BRANCH_WIDTH), lambda j, i: (i, 0)),
                  pl.BlockSpec((N_BRANCH, BRANCH_WIDTH, tn), lambda j, i: (0, 0, j)),
                  mp_spec(0), mp_spec(1), mp_spec(2)],
        out_specs=pl.BlockSpec((tm, tn), lambda j, i: (i, j)),
        out_shape=jax.ShapeDtypeStruct((m, D_MODEL), BF16),
        compiler_params=_params("parallel", "parallel"),
        name="branch_mix",
    )(y, w_branch, proj2, proj2, proj2)


def _out_proj_body(a_ref, w_ref, x_ref, o_ref):
    o_ref[...] = x_ref[...] + jnp.dot(a_ref[...], w_ref[...], preferred_element_type=F32)


def _out_proj(a, w, x, tm=1024, tn=1024):
    m, k = a.shape
    n = w.shape[1]
    return pl.pallas_call(
        _out_proj_body,
        grid=(n // tn, m // tm),
        in_specs=[pl.BlockSpec((tm, k), lambda j, i: (i, 0)), pl.BlockSpec((k, tn), lambda j, i: (0, j)),
                  pl.BlockSpec((tm, tn), lambda j, i: (i, j))],
        out_specs=pl.BlockSpec((tm, tn), lambda j, i: (i, j)),
        out_shape=jax.ShapeDtypeStruct((m, n), F32),
        compiler_params=_params("parallel", "parallel"),
        name="out_proj",
    )(a, w, x)


def _router_body(x_ref, g_ref, w_ref, h_ref, aff_ref):
    x = x_ref[...]
    h = x * lax.rsqrt(jnp.mean(x * x, axis=-1, keepdims=True) + RMS_EPS) * g_ref[...]
    h_ref[...] = h.astype(h_ref.dtype)
    logits = jnp.dot(h, w_ref[...], preferred_element_type=F32, precision=lax.Precision.HIGHEST)
    e = jnp.exp(logits - jnp.max(logits, axis=-1, keepdims=True))
    aff_ref[...] = e / jnp.sum(e, axis=-1, keepdims=True)


def _router(x, g, w_router, tm=512):
    m, d = x.shape
    return pl.pallas_call(
        _router_body,
        grid=(m // tm,),
        in_specs=[pl.BlockSpec((tm, d), lambda i: (i, 0)), pl.BlockSpec((1, d), lambda i: (0, 0)),
                  pl.BlockSpec((d, N_EXPERTS), lambda i: (0, 0))],
        out_specs=[pl.BlockSpec((tm, d), lambda i: (i, 0)), pl.BlockSpec((tm, N_EXPERTS), lambda i: (i, 0))],
        out_shape=[jax.ShapeDtypeStruct((m, d), BF16), jax.ShapeDtypeStruct((m, N_EXPERTS), F32)],
        compiler_params=_params("parallel"),
        name="moe_router",
    )(x, g.reshape(1, d), w_router)


def _route_body(aff_ref, slot_ref, *, cap):
    aff = aff_ref[0]
    e, s = aff.shape
    thr = jnp.zeros((e, 1), jnp.int32)
    for bit in range(30, -1, -1):
        cand = thr | (1 << bit)
        cnt = jnp.sum(jnp.where(aff >= pltpu.bitcast(cand, F32), 1.0, 0.0), axis=1, keepdims=True)
        thr = jnp.where(cnt >= cap, cand, thr)
    thr_f = pltpu.bitcast(thr, F32)
    gt = aff > thr_f
    eq = aff == thr_f
    n_gt = jnp.sum(jnp.where(gt, 1.0, 0.0), axis=1, keepdims=True)
    upper = (lax.broadcasted_iota(jnp.int32, (s, s), 0)
             <= lax.broadcasted_iota(jnp.int32, (s, s), 1)).astype(BF16)
    eq_rank = jnp.dot(jnp.where(eq, 1.0, 0.0).astype(BF16), upper, preferred_element_type=F32)
    sel = gt | (eq & (eq_rank <= cap - n_gt))
    sel_f = jnp.where(sel, 1.0, 0.0)
    pos = jnp.dot(sel_f.astype(BF16), upper, preferred_element_type=F32)
    slot_ref[0] = jnp.where(sel, pos - 1.0, -1.0).astype(jnp.int32)


def _route(aff_t, cap):
    b, e, s = aff_t.shape
    return pl.pallas_call(
        functools.partial(_route_body, cap=cap),
        grid=(b,),
        in_specs=[pl.BlockSpec((1, e, s), lambda i: (i, 0, 0))],
        out_specs=pl.BlockSpec((1, e, s), lambda i: (i, 0, 0)),
        out_shape=jax.ShapeDtypeStruct((b, e, s), jnp.int32),
        compiler_params=_params("parallel"),
        name="moe_route",
    )(aff_t)


def _gather_body(h_ref, slot_ref, aff_ref, xe_ref, gate_ref, *, cap):
    e = pl.program_id(1)
    slot = slot_ref[0, pl.ds(e, 1), :]
    aff = aff_ref[0, pl.ds(e, 1), :]
    s = slot.shape[1]
    onehot = lax.broadcasted_iota(jnp.int32, (cap, s), 0) == slot
    xe = jnp.dot(jnp.where(onehot, 1.0, 0.0).astype(BF16), h_ref[0], preferred_element_type=F32)
    xe_ref[0, 0] = xe.astype(xe_ref.dtype)
    gate = jnp.sum(jnp.where(onehot, aff, 0.0), axis=1, keepdims=True)
    gate_ref[0, 0] = jnp.broadcast_to(gate, (cap, HEAD_DIM))


def _gather(h, slot_t, aff_t, cap):
    b, s, d = h.shape
    e = slot_t.shape[1]
    return pl.pallas_call(
        functools.partial(_gather_body, cap=cap),
        grid=(b, e),
        in_specs=[pl.BlockSpec((1, s, d), lambda i, j: (i, 0, 0)),
                  pl.BlockSpec((1, e, s), lambda i, j: (i, 0, 0)),
                  pl.BlockSpec((1, e, s), lambda i, j: (i, 0, 0))],
        out_specs=[pl.BlockSpec((1, 1, cap, d), lambda i, j: (j, i, 0, 0)),
                   pl.BlockSpec((1, 1, cap, HEAD_DIM), lambda i, j: (j, i, 0, 0))],
        out_shape=[jax.ShapeDtypeStruct((e, b, cap, d), BF16),
                   jax.ShapeDtypeStruct((e, b, cap, HEAD_DIM), F32)],
        compiler_params=_params("parallel", "arbitrary"),
        name="moe_gather",
    )(h, slot_t, aff_t)


def _expert_body(x_ref, wg_ref, wu_ref, wd_ref, gate_ref, y_ref, hid_ref, *, nf):
    f = pl.program_id(1)
    nb, cap, d = x_ref.shape[1:]
    tf = hid_ref.shape[2]

    @pl.when(f < nf)
    def _():
        x = x_ref[0].reshape(nb * cap, d)
        hg = jnp.dot(x, wg_ref[0, 0].astype(BF16), preferred_element_type=F32)
        hu = jnp.dot(x, wu_ref[0, 0].astype(BF16), preferred_element_type=F32)
        hid_ref[jnp.minimum(f, nf - 1)] = (hg * jax.nn.sigmoid(hg) * hu).astype(BF16)

    @pl.when(f >= nf)
    def _():
        acc = None
        for c in range(nf):
            part = jnp.dot(hid_ref[c], wd_ref[0, 0, c * tf:(c + 1) * tf, :].astype(BF16), preferred_element_type=F32)
            acc = part if acc is None else acc + part
        gate = gate_ref[0].reshape(nb * cap, HEAD_DIM)[:, :1]
        y_ref[:, 0] = (acc * gate).astype(y_ref.dtype).reshape(nb, cap, acc.shape[1])


def _experts(xe, gates, w_gate, w_up, w_down, layer, tf=512, tn=1024):
    e, b, cap, d = xe.shape
    ff = w_gate.shape[3]
    nf, nd = ff // tf, d // tn
    up = lambda f: jnp.minimum(f, nf - 1)
    down = lambda f: jnp.maximum(f - nf, 0)
    return pl.pallas_call(
        functools.partial(_expert_body, nf=nf),
        grid=(e, nf + nd),
        in_specs=[pl.BlockSpec((1, b, cap, d), lambda i, f: (i, 0, 0, 0)),
                  pl.BlockSpec((1, 1, d, tf), lambda i, f: (layer, i, 0, up(f))),
                  pl.BlockSpec((1, 1, d, tf), lambda i, f: (layer, i, 0, up(f))),
                  pl.BlockSpec((1, 1, ff, tn), lambda i, f: (layer, i, 0, down(f))),
                  pl.BlockSpec((1, b, cap, HEAD_DIM), lambda i, f: (i, 0, 0, 0))],
        out_specs=pl.BlockSpec((b, 1, cap, tn), lambda i, f: (0, i, 0, down(f))),
        out_shape=jax.ShapeDtypeStruct((b, e, cap, d), BF16),
        scratch_shapes=[pltpu.VMEM((nf, b * cap, tf), BF16)],
        compiler_params=_params("parallel", "arbitrary"),
        name="moe_experts",
    )(xe, w_gate, w_up, w_down, gates)


def _combine_body(x_ref, slot_ref, ye_ref, g_ref, o_ref, *, final_norm):
    ne, cap, d = ye_ref.shape[1:]
    slot = slot_ref[0]
    lane = lax.broadcasted_iota(jnp.int32, (slot.shape[0], cap), 1)
    onehot = jnp.concatenate(
        [jnp.where(slot[:, e:e + 1] == lane, 1.0, 0.0).astype(BF16) for e in range(ne)], axis=-1)
    out = x_ref[0] + jnp.dot(onehot, ye_ref[0].reshape(ne * cap, d), preferred_element_type=F32)
    if final_norm:
        out = out * lax.rsqrt(jnp.mean(out * out, axis=-1, keepdims=True) + RMS_EPS) * g_ref[...]
    o_ref[0] = out


def _combine(x, slot_nat, ye, final_g, final_norm, ts=256):
    b, s, d = x.shape
    _, e, cap, _ = ye.shape
    return pl.pallas_call(
        functools.partial(_combine_body, final_norm=final_norm),
        grid=(b, s // ts),
        in_specs=[pl.BlockSpec((1, ts, d), lambda i, j: (i, j, 0)),
                  pl.BlockSpec((1, ts, e), lambda i, j: (i, j, 0)),
                  pl.BlockSpec((1, e, cap, d), lambda i, j: (i, 0, 0, 0)),
                  pl.BlockSpec((1, d), lambda i, j: (0, 0))],
        out_specs=pl.BlockSpec((1, ts, d), lambda i, j: (i, j, 0)),
        out_shape=jax.ShapeDtypeStruct((b, s, d), F32),
        compiler_params=_params("parallel", "arbitrary"),
        name="moe_combine",
    )(x, slot_nat, ye, final_g.reshape(1, d))


def _split_w_in(w):
    bw = BRANCH_WIDTH
    sizes = (3 * bw, bw, 4 * N_HEADS, 3 * bw, bw, 4 * N_HEADS, 3 * bw, N_BRANCH * D_MODEL)
    offs = [0]
    for sz in sizes:
        offs.append(offs[-1] + sz)
    a_qkv, a_z, a_gate, b_qkv, b_o, b_gate, c_qkv, merge = [w[:, offs[i]:offs[i + 1]] for i in range(8)]
    main = jnp.concatenate([merge, a_qkv, a_z, b_qkv, b_o, c_qkv], axis=1).astype(BF16)
    pad = jnp.zeros((w.shape[0], GATE_LANES - 8 * N_HEADS), w.dtype)
    return main, jnp.concatenate([a_gate, b_gate, pad], axis=1).astype(BF16)


def _mixer(x2, b, s, norm_g, w_in, conv_w, a_log, dt_bias, gdn_g, i_bias, f_bias, mlstm_g, w_branch, w_out,
           rope):
    h = _rmsnorm(x2, norm_g, BF16)
    w_main, w_gates = _split_w_in(w_in)
    proj2 = _in_proj(h, w_main)
    g_nat, g_t = _gate_proj(h, w_gates)
    proj = proj2.reshape(b, s, N_PROJ)
    o_a = _gdn_chunks(_gdn_prep(proj, conv_w), g_nat, g_t, a_log, dt_bias)
    o_b = _mlstm_chunks(proj, g_nat, g_t, i_bias, f_bias)
    swa = [_swa_group(proj, rope[0], rope[1], gi) for gi in range(len(SWA_GROUPS))]
    y = _merge(o_a, o_b, proj2, jnp.tile(gdn_g, N_HEADS).reshape(1, BRANCH_WIDTH),
               mlstm_g.reshape(1, BRANCH_WIDTH), [o for o, _ in swa], [l for _, l in swa])
    mixed = _branch_mix(y, w_branch.astype(BF16), proj2)
    return _out_proj(mixed, w_out.astype(BF16), x2)


def _moe(x2, b, s, norm_g, w_router, w_gate, w_up, w_down, layer, final_g, final_norm):
    cap = EC_CAPACITY_FACTOR * s // N_EXPERTS
    h, aff = _router(x2, norm_g, w_router)
    aff_t = jnp.swapaxes(aff.reshape(b, s, N_EXPERTS), 1, 2)
    slot_t = _route(aff_t, cap)
    xe, gates = _gather(h.reshape(b, s, D_MODEL), slot_t, aff_t, cap)
    ye = _experts(xe, gates, w_gate, w_up, w_down, layer)
    out = _combine(x2.reshape(b, s, D_MODEL), jnp.swapaxes(slot_t, 1, 2), ye, final_g, final_norm)
    return out.reshape(b * s, D_MODEL)


def kernel(x, norm1_g, w_in, conv_w, gdn_a_log, gdn_dt_bias, gdn_norm_g, mlstm_i_bias, mlstm_f_bias, mlstm_norm_g, w_branch, w_out, norm2_g, w_router, w_gate, w_up, w_down, final_norm_g):
    b, s, d = x.shape
    rope = _rope_tables(s)
    x2 = x.reshape(b * s, d)
    for layer in range(DEPTH):
        x2 = _mixer(x2, b, s, norm1_g[layer], w_in[layer], conv_w[layer], gdn_a_log[layer], gdn_dt_bias[layer],
                    gdn_norm_g[layer], mlstm_i_bias[layer], mlstm_f_bias[layer], mlstm_norm_g[layer],
                    w_branch[layer], w_out[layer], rope)
        x2 = _moe(x2, b, s, norm2_g[layer], w_router[layer], w_gate, w_up, w_down, layer,
                  final_norm_g, layer == DEPTH - 1)
    return x2.reshape(b, s, d)
```

```python
import functools

import jax
import jax.numpy as jnp
from jax import lax
from jax.experimental import pallas as pl
from jax.experimental.pallas import tpu as pltpu

F32 = jnp.float32
BF16 = jnp.bfloat16

D_MODEL = 2048
DEPTH = 2
HEAD_DIM = 128
N_HEADS = 6
BRANCH_WIDTH = N_HEADS * HEAD_DIM
CONV_WIDTH = 5
CHUNK = 64
GROUP = 4
ROWS = GROUP * CHUNK
MLSTM_ROWS = 256
SWA_GROUPS = ((128, 1), (512, 4), (2048, 16))
SWA_HALF = 64
SWA_TQ = 128
ROPE_THETA = 500000.0
ROPE_DIM = HEAD_DIM // 4
N_BRANCH = 3
N_EXPERTS = 16
EXPERT_FF = 1024
EC_CAPACITY_FACTOR = 2
RMS_EPS = 1e-6
NEG_INF = -1e30

SPAN_A = (0, 4 * BRANCH_WIDTH)
SPAN_GATE_A = (SPAN_A[0] + SPAN_A[1], 4 * N_HEADS)
SPAN_B = (SPAN_GATE_A[0] + SPAN_GATE_A[1], 4 * BRANCH_WIDTH)
SPAN_GATE_B = (SPAN_B[0] + SPAN_B[1], 4 * N_HEADS)
SPAN_C = (SPAN_GATE_B[0] + SPAN_GATE_B[1], 3 * BRANCH_WIDTH)
SPAN_MERGE = (SPAN_C[0] + SPAN_C[1], N_BRANCH * D_MODEL)
LANES = 128
GATE_LANES = 128
GATE_A = 0
GATE_B = 4 * N_HEADS

VMEM_LIMIT_BYTES = 56 * 1024 * 1024


def _params(*sem):
    return pltpu.CompilerParams(dimension_semantics=sem, vmem_limit_bytes=VMEM_LIMIT_BYTES)


def _dot(a, b):
    return jnp.dot(a.astype(BF16), b.astype(BF16), preferred_element_type=F32)


def _dot_nt(a, b):
    return lax.dot_general(a.astype(BF16), b.astype(BF16), (((1,), (1,)), ((), ())),
                           preferred_element_type=F32)


def _dot_tn(a, b):
    return lax.dot_general(a.astype(BF16), b.astype(BF16), (((0,), (0,)), ((), ())),
                           preferred_element_type=F32)


def _rmsnorm_body(x_ref, g_ref, o_ref):
    x = x_ref[...]
    y = x * lax.rsqrt(jnp.mean(x * x, axis=-1, keepdims=True) + RMS_EPS)
    o_ref[...] = (y * g_ref[...]).astype(o_ref.dtype)


def _rmsnorm(x, g, out_dtype, tm=512):
    m, d = x.shape
    return pl.pallas_call(
        _rmsnorm_body,
        grid=(m // tm,),
        in_specs=[pl.BlockSpec((tm, d), lambda i: (i, 0)), pl.BlockSpec((1, d), lambda i: (0, 0))],
        out_specs=pl.BlockSpec((tm, d), lambda i: (i, 0)),
        out_shape=jax.ShapeDtypeStruct((m, d), out_dtype),
        compiler_params=_params("parallel"),
        name="rmsnorm",
    )(x, g.reshape(1, d))


def _in_proj_body(x_ref, *refs, delta):
    *chunks, o_ref, w_bf = refs
    tn = w_bf.shape[0]

    @pl.when(pl.program_id(1) == 0)
    def _():
        tall = chunks[0].shape[1]
        for c in range(tn // tall):
            if delta:
                piece = jnp.concatenate([chunks[c][0], chunks[c + 1][0]], axis=0)[delta:delta + tall]
            else:
                piece = chunks[c][0]
            w_bf[c * tall:(c + 1) * tall, :] = piece.astype(BF16)

    o_ref[...] = _dot_nt(x_ref[...], w_bf[...]).astype(o_ref.dtype)


def _in_proj(h, w_t, layer, span, tm=2048, tn=BRANCH_WIDTH):
    start, width = span
    m, k = h.shape
    tall = 2 * LANES
    base, delta = divmod(start, tall)
    per_tile = tn // tall
    n_chunks = per_tile + (1 if delta else 0)
    chunk_specs = [pl.BlockSpec((1, tall, k), lambda j, i, c=c: (layer, base + per_tile * j + c, 0))
                   for c in range(n_chunks)]
    return pl.pallas_call(
        functools.partial(_in_proj_body, delta=delta),
        grid=(width // tn, m // tm),
        in_specs=[pl.BlockSpec((tm, k), lambda j, i: (i, 0))] + chunk_specs,
        out_specs=pl.BlockSpec((tm, tn), lambda j, i: (i, j)),
        out_shape=jax.ShapeDtypeStruct((m, width), BF16),
        scratch_shapes=[pltpu.VMEM((tn, k), BF16)],
        compiler_params=_params("parallel", "arbitrary"),
        name="in_proj",
    )(h, *([w_t] * n_chunks))


def _gate_proj_body(x_ref, wa_ref, wb_ref, nat_ref, t_ref):
    row = lax.broadcasted_iota(jnp.int32, wa_ref.shape[1:], 0)
    w = jnp.where(row < GATE_B, wa_ref[0], jnp.where(row < 2 * GATE_B, wb_ref[0], 0.0))
    g = _dot_nt(x_ref[...], w)
    nat_ref[...] = g
    t_ref[...] = g.T


def _gate_proj(h, w_t, layer, tm=512):
    m, k = h.shape
    (chunk_a, row_a), (chunk_b, row_b) = divmod(SPAN_GATE_A[0], LANES), divmod(SPAN_GATE_B[0], LANES)
    assert (row_a, row_b) == (GATE_A, GATE_B) and GATE_LANES == LANES
    return pl.pallas_call(
        _gate_proj_body,
        grid=(m // tm,),
        in_specs=[pl.BlockSpec((tm, k), lambda i: (i, 0)),
                  pl.BlockSpec((1, LANES, k), lambda i: (layer, chunk_a, 0)),
                  pl.BlockSpec((1, LANES, k), lambda i: (layer, chunk_b, 0))],
        out_specs=[pl.BlockSpec((tm, GATE_LANES), lambda i: (i, 0)), pl.BlockSpec((GATE_LANES, tm), lambda i: (0, i))],
        out_shape=[jax.ShapeDtypeStruct((m, GATE_LANES), F32), jax.ShapeDtypeStruct((GATE_LANES, m), F32)],
        compiler_params=_params("parallel"),
        name="gate_proj",
    )(h, w_t, w_t)


def _gdn_prep_body(x_ref, w_ref, o_ref):
    x = x_ref[0].astype(F32)
    w = w_ref[...]
    s = x.shape[0]
    row = lax.broadcasted_iota(jnp.int32, x.shape, 0)
    pad = (CONV_WIDTH - 1) // 2
    acc = x * w[pad:pad + 1]
    for tap in range(CONV_WIDTH):
        off = tap - pad
        if off == 0:
            continue
        shifted = pltpu.roll(x, (-off) % s, 0)
        valid = (row + off >= 0) & (row + off < s)
        acc = acc + jnp.where(valid, shifted, 0.0) * w[tap:tap + 1]
    y = acc * jax.nn.sigmoid(acc)
    r = lax.rsqrt(jnp.sum(y * y, axis=-1, keepdims=True) + RMS_EPS)
    j = pl.program_id(1)
    fac = jnp.where(j < N_HEADS, r * (HEAD_DIM ** -0.5), jnp.where(j < 2 * N_HEADS, r, 1.0))
    o_ref[0] = (y * fac).astype(o_ref.dtype)


def _gdn_prep(proj, conv_w):
    b, s, _ = proj.shape
    nblk = 3 * N_HEADS
    return pl.pallas_call(
        _gdn_prep_body,
        grid=(b, nblk),
        in_specs=[pl.BlockSpec((1, s, HEAD_DIM), lambda i, j: (i, 0, j)),
                  pl.BlockSpec((CONV_WIDTH, HEAD_DIM), lambda i, j: (0, j))],
        out_specs=pl.BlockSpec((1, s, HEAD_DIM), lambda i, j: (i, 0, j)),
        out_shape=jax.ShapeDtypeStruct((b, s, 3 * BRANCH_WIDTH), BF16),
        compiler_params=_params("parallel", "parallel"),
        name="gdn_prep",
    )(proj, conv_w)


def _group_masks(direction, n):
    ii = lax.broadcasted_iota(jnp.int32, (n, n), 0)
    jj = lax.broadcasted_iota(jnp.int32, (n, n), 1)
    shift = CHUNK.bit_length() - 1
    blk = (ii >> shift) == (jj >> shift)
    rel = (ii - jj) if direction == 0 else (jj - ii)
    return blk, blk & (rel >= 0), blk & (rel > 0), blk & (rel <= 0)


def _softplus(x):
    return jnp.maximum(x, 0.0) + jnp.log1p(jnp.exp(-jnp.abs(x)))


def _log_sigmoid(x):
    return -_softplus(-x)


def _chunk_order(direction):
    return range(GROUP) if direction == 0 else range(GROUP - 1, -1, -1)


def _bidir_specs(b, s, col0, rows):
    ng = s // rows

    def specs(grp):
        qkv = [pl.BlockSpec((1, rows, BRANCH_WIDTH), lambda i, g, c=c: (i, grp(g), col0 + c)) for c in range(3)]
        return qkv + [pl.BlockSpec((rows, GATE_LANES), lambda i, g: (i * ng + grp(g), 0)),
                      pl.BlockSpec((GATE_LANES, rows), lambda i, g: (0, i * ng + grp(g)))]

    fwd, bwd = (lambda g: g), (lambda g: ng - 1 - g)
    out = [pl.BlockSpec((1, rows, BRANCH_WIDTH), lambda i, g: (i, fwd(g), 0)),
           pl.BlockSpec((1, rows, BRANCH_WIDTH), lambda i, g: (i, bwd(g), 0))]
    return specs(fwd) + specs(bwd), out


def _gdn_body(qf, kf, vf, gnf, gtf, qb, kb, vb, gnb, gtb, al_ref, dt_ref, of, ob, state):
    @pl.when(pl.program_id(1) == 0)
    def _():
        state[...] = jnp.zeros_like(state)

    pair = 2 * CHUNK
    halves = ROWS // pair
    eye = (lax.broadcasted_iota(jnp.int32, (pair, pair), 0)
           == lax.broadcasted_iota(jnp.int32, (pair, pair), 1)).astype(F32)
    streams = ((qf, kf, vf, gnf, gtf, of), (qb, kb, vb, gnb, gtb, ob))
    heads = range(N_HEADS)
    sls = [slice(h * HEAD_DIM, (h + 1) * HEAD_DIM) for h in heads]
    items = [(h, slice(hf * pair, (hf + 1) * pair)) for h in heads for hf in range(halves)]
    n = range(len(items))
    for d, (q_ref, k_ref, v_ref, gn_ref, gt_ref, o_ref) in enumerate(streams):
        blk, incl, strict, incl_t = _group_masks(d, pair)
        gn = gn_ref[...]
        gt = gt_ref[...]
        cg = [GATE_A + d * N_HEADS + h for h in heads]
        cb = [c + 2 * N_HEADS for c in cg]
        neg_a = [-jnp.exp(al_ref[d:d + 1, h:h + 1]) for h in heads]
        dtb = [dt_ref[d:d + 1, h:h + 1] for h in heads]
        g_c = [neg_a[h] * _softplus(gn[rw, cg[h]:cg[h] + 1] + dtb[h]) for h, rw in items]
        g_r = [neg_a[h] * _softplus(gt[cg[h]:cg[h] + 1, rw] + dtb[h]) for h, rw in items]
        beta = [jax.nn.sigmoid(gn[rw, cb[h]:cb[h] + 1]) for h, rw in items]
        gc_c = [jnp.sum(jnp.where(incl, g_r[i], 0.0), axis=1, keepdims=True) for i in n]
        gc_r = [jnp.sum(jnp.where(incl_t, g_c[i], 0.0), axis=0, keepdims=True) for i in n]
        gtot = [jnp.sum(jnp.where(blk, g_r[i], 0.0), axis=1, keepdims=True) for i in n]
        decay = [jnp.where(incl, jnp.exp(jnp.where(incl, gc_c[i] - gc_r[i], 0.0)), 0.0) for i in n]
        q = [q_ref[0, rw, sls[h]].astype(F32) for h, rw in items]
        k = [k_ref[0, rw, sls[h]].astype(F32) for h, rw in items]
        v = [v_ref[0, rw, sls[h]].astype(F32) for h, rw in items]
        gram = [_dot_nt(jnp.concatenate([k[i], q[i]], axis=0), k[i]) for i in n]
        p = [-jnp.where(strict, beta[i] * decay[i] * gram[i][:pair], 0.0) for i in n]
        t_inv = [eye + p[i] for i in n]
        p = [_dot(p[i], p[i]) for i in n]
        for _ in range(4):
            both = [_dot(p[i], jnp.concatenate([t_inv[i], p[i]], axis=1)) for i in n]
            t_inv = [t_inv[i] + both[i][:, :pair] for i in n]
            p = [both[i][:, pair:] for i in n]
        t_inv = [t_inv[i] + _dot(p[i], t_inv[i]) for i in n]
        e_c = [jnp.exp(gc_c[i]) for i in n]
        sol = [_dot(t_inv[i], jnp.concatenate([v[i] * beta[i], k[i] * (beta[i] * e_c[i])], axis=-1)) for i in n]
        w_v = [sol[i][:, :HEAD_DIM] for i in n]
        w_k = [sol[i][:, HEAD_DIM:] for i in n]
        qkd = [gram[i][pair:] * decay[i] for i in n]
        q_dec = [q[i] * e_c[i] for i in n]
        k_dec = [k[i] * jnp.exp(gtot[i] - gc_c[i]) for i in n]
        c_dec = [jnp.exp(gtot[i]) for i in n]
        st = [state[d, h] for h in heads]
        for c in _chunk_order(d):
            hf, lc = divmod(c, pair // CHUNK)
            lr = slice(lc * CHUNK, (lc + 1) * CHUNK)
            r = slice(c * CHUNK, (c + 1) * CHUNK)
            it = [h * halves + hf for h in heads]
            xs = [_dot(jnp.concatenate([w_k[i][lr], q_dec[i][lr]], axis=0), st[h]) for h, i in zip(heads, it)]
            u = [w_v[i][lr] - xs[h][:CHUNK] for h, i in zip(heads, it)]
            for h, i in zip(heads, it):
                o_ref[0, r, sls[h]] = xs[h][CHUNK:] + _dot(qkd[i][lr, lr], u[h])
            st = [c_dec[i][lc * CHUNK:lc * CHUNK + 1] * st[h] + _dot_tn(k_dec[i][lr], u[h])
                  for h, i in zip(heads, it)]
        for h in heads:
            state[d, h] = st[h]


def _gdn_chunks(qkv, g_nat, g_t, a_log, dt_bias):
    b, s, _ = qkv.shape
    in_specs, out_specs = _bidir_specs(b, s, 0, ROWS)
    small = pl.BlockSpec((2, N_HEADS), lambda i, g: (0, 0))
    sds = jax.ShapeDtypeStruct((b, s, BRANCH_WIDTH), F32)
    ins = [qkv, qkv, qkv, g_nat, g_t]
    return pl.pallas_call(
        _gdn_body,
        grid=(b, s // ROWS),
        in_specs=in_specs + [small, small],
        out_specs=out_specs,
        out_shape=[sds, sds],
        scratch_shapes=[pltpu.VMEM((2, N_HEADS, HEAD_DIM, HEAD_DIM), F32)],
        compiler_params=_params("parallel", "arbitrary"),
        name="gdn_chunks",
    )(*ins, *ins, a_log, dt_bias)


def _mlstm_body(qf, kf, vf, gnf, gtf, qb, kb, vb, gnb, gtb, ib_ref, fb_ref, of, ob, c_state, n_state, m_state):
    @pl.when(pl.program_id(1) == 0)
    def _():
        c_state[...] = jnp.zeros_like(c_state)
        n_state[...] = jnp.zeros_like(n_state)
        m_state[...] = jnp.zeros_like(m_state)

    ii = lax.broadcasted_iota(jnp.int32, (MLSTM_ROWS, MLSTM_ROWS), 0)
    jj = lax.broadcasted_iota(jnp.int32, (MLSTM_ROWS, MLSTM_ROWS), 1)
    streams = ((qf, kf, vf, gnf, gtf, of), (qb, kb, vb, gnb, gtb, ob))
    for d, (q_ref, k_ref, v_ref, gn_ref, gt_ref, o_ref) in enumerate(streams):
        incl = (jj <= ii) if d == 0 else (jj >= ii)
        incl_t = (ii <= jj) if d == 0 else (ii >= jj)
        gn = gn_ref[...]
        gt = gt_ref[...]
        heads = range(N_HEADS)
        sls = [slice(h * HEAD_DIM, (h + 1) * HEAD_DIM) for h in heads]
        ci = [GATE_B + d * N_HEADS + h for h in heads]
        cf = [c + 2 * N_HEADS for c in ci]
        ib = [ib_ref[d:d + 1, h:h + 1] for h in heads]
        fb = [fb_ref[d:d + 1, h:h + 1] for h in heads]
        q = [q_ref[0, :, sl].astype(F32) for sl in sls]
        v = [v_ref[0, :, sl].astype(F32) for sl in sls]
        k = [k_ref[0, :, sl].astype(F32) * (HEAD_DIM ** -0.5) for sl in sls]
        qk = [_dot_nt(q[h], k[h]) for h in heads]
        cmat = [c_state[d, h] for h in heads]
        nvec = [n_state[d, h] for h in heads]
        m0 = [m_state[d, h][:, :1] for h in heads]
        qc = [_dot(q[h], cmat[h]) for h in heads]
        i_c = [gn[:, ci[h]:ci[h] + 1] + ib[h] for h in heads]
        i_r = [gt[ci[h]:ci[h] + 1, :] + ib[h] for h in heads]
        f_c = [_log_sigmoid(gn[:, cf[h]:cf[h] + 1] + fb[h]) for h in heads]
        f_r = [_log_sigmoid(gt[cf[h]:cf[h] + 1, :] + fb[h]) for h in heads]
        b_c = [jnp.sum(jnp.where(incl, f_r[h], 0.0), axis=1, keepdims=True) for h in heads]
        b_r = [jnp.sum(jnp.where(incl_t, f_c[h], 0.0), axis=0, keepdims=True) for h in heads]
        btot = [jnp.sum(f_c[h], axis=0, keepdims=True) for h in heads]
        log_d = [jnp.where(incl, b_c[h] - b_r[h] + i_r[h], NEG_INF) for h in heads]
        row_max = [jnp.max(log_d[h], axis=1, keepdims=True) for h in heads]
        lend = [btot[h] - b_c[h] + i_c[h] for h in heads]
        m_t = [jnp.maximum(b_c[h] + m0[h], row_max[h]) for h in heads]
        inter = [jnp.exp(b_c[h] + m0[h] - m_t[h]) for h in heads]
        dmat = [jnp.exp(log_d[h] - m_t[h]) * qk[h] for h in heads]
        dv = [_dot(dmat[h], v[h]) for h in heads]
        m_new = [jnp.maximum(btot[h] + m0[h], jnp.max(lend[h], axis=0, keepdims=True)) for h in heads]
        scale = [jnp.exp(btot[h] + m0[h] - m_new[h]) for h in heads]
        kw = [k[h] * jnp.exp(lend[h] - m_new[h]) for h in heads]
        kv = [_dot_tn(kw[h], v[h]) for h in heads]
        for h in heads:
            den = (inter[h] * jnp.sum(q[h] * nvec[h], axis=1, keepdims=True)
                   + jnp.sum(dmat[h], axis=1, keepdims=True))
            o_ref[0, :, sls[h]] = (inter[h] * qc[h] + dv[h]) / jnp.maximum(jnp.abs(den), jnp.exp(-m_t[h]))
            c_state[d, h] = scale[h] * cmat[h] + kv[h]
            n_state[d, h] = scale[h] * nvec[h] + jnp.sum(kw[h], axis=0, keepdims=True)
            m_state[d, h] = jnp.broadcast_to(m_new[h], (1, HEAD_DIM))


def _mlstm_chunks(proj, g_nat, g_t, i_bias, f_bias):
    b, s, _ = proj.shape
    in_specs, out_specs = _bidir_specs(b, s, 0, MLSTM_ROWS)
    small = pl.BlockSpec((2, N_HEADS), lambda i, g: (0, 0))
    sds = jax.ShapeDtypeStruct((b, s, BRANCH_WIDTH), F32)
    ins = [proj, proj, proj, g_nat, g_t]
    return pl.pallas_call(
        _mlstm_body,
        grid=(b, s // MLSTM_ROWS),
        in_specs=in_specs + [small, small],
        out_specs=out_specs,
        out_shape=[sds, sds],
        scratch_shapes=[pltpu.VMEM((2, N_HEADS, HEAD_DIM, HEAD_DIM), F32),
                        pltpu.VMEM((2, N_HEADS, 1, HEAD_DIM), F32),
                        pltpu.VMEM((2, N_HEADS, 1, HEAD_DIM), F32)],
        compiler_params=_params("parallel", "arbitrary"),
        name="mlstm_chunks",
    )(*ins, *ins, i_bias, f_bias)


def _rope_tables(s):
    half = ROPE_DIM // 2
    inv_freq = jnp.power(ROPE_THETA, -jnp.arange(half, dtype=F32) * (2.0 / ROPE_DIM))
    ang = jnp.arange(s, dtype=F32)[:, None] * inv_freq[None, :]
    cos, sin = jnp.cos(ang), jnp.sin(ang)
    rest = HEAD_DIM - ROPE_DIM
    cos_t = jnp.concatenate([cos, cos, jnp.ones((s, rest), F32)], axis=-1)
    sin_t = jnp.concatenate([-sin, sin, jnp.zeros((s, rest), F32)], axis=-1)
    return cos_t, sin_t


def _rotary(x, cos_t, sin_t):
    half = ROPE_DIM // 2
    lane = lax.broadcasted_iota(jnp.int32, x.shape, 1)
    swapped = jnp.where(lane < half, pltpu.roll(x, HEAD_DIM - half, 1), pltpu.roll(x, half, 1))
    return x * cos_t + swapped * sin_t


def _swa_body(q_ref, k_ref, v_ref, cos_ref, sin_ref, o_ref, lse_ref, q_rot, k_rot, v_f32, *, dil):
    s_len = q_ref.shape[1]
    t = s_len // dil
    win = min(t, SWA_TQ + 2 * SWA_HALF)
    cos_t, sin_t = cos_ref[...], sin_ref[...]
    q_rot[...] = _rotary(q_ref[0].astype(F32), cos_t, sin_t) * (HEAD_DIM ** -0.5)
    k_rot[...] = _rotary(k_ref[0].astype(F32), cos_t, sin_t)
    v_f32[...] = v_ref[0].astype(F32)

    def rows(first, count):
        return pl.ds(first, count, stride=dil) if dil > 1 else pl.ds(first, count)

    blocks = [(r, tb) for r in range(dil) for tb in range(t // SWA_TQ)]
    batch = 4
    for b0 in range(0, len(blocks), batch):
        grp = blocks[b0:b0 + batch]
        starts = [min(max(tb * SWA_TQ - SWA_HALF, 0), t - win) for _, tb in grp]
        q_rows = [rows(r + tb * SWA_TQ * dil, SWA_TQ) for r, tb in grp]
        kv_rows = [rows(r + st * dil, win) for (r, _), st in zip(grp, starts)]
        sc = [_dot_nt(q_rot[qr, :], k_rot[kr, :]) for qr, kr in zip(q_rows, kv_rows)]
        ms, ps, dens = [], [], []
        for (_, tb), st, s in zip(grp, starts, sc):
            rel = (tb * SWA_TQ - st) + (lax.broadcasted_iota(jnp.int32, s.shape, 0)
                                        - lax.broadcasted_iota(jnp.int32, s.shape, 1))
            s = jnp.where(jnp.abs(rel) <= SWA_HALF, s, NEG_INF)
            m = jnp.max(s, axis=1, keepdims=True)
            p = jnp.exp(s - m)
            ms.append(m)
            ps.append(p)
            dens.append(jnp.sum(p, axis=1, keepdims=True))
        pv = [_dot(p, v_f32[kr, :]) for p, kr in zip(ps, kv_rows)]
        for qr, o, m, den in zip(q_rows, pv, ms, dens):
            o_ref[0, qr, :] = o / den
            lse_ref[0, qr, :] = jnp.broadcast_to(m + jnp.log(den), (SWA_TQ, HEAD_DIM))


def _swa_group(proj, cos_t, sin_t, gi):
    b, s, _ = proj.shape
    dil = SWA_GROUPS[gi][1]
    qcol = 2 * gi
    kcol, vcol = qcol + N_HEADS, qcol + 2 * N_HEADS
    table = pl.BlockSpec((s, HEAD_DIM), lambda i, j: (0, 0))
    out_spec = pl.BlockSpec((1, s, HEAD_DIM), lambda i, j: (i, 0, j))
    out_sds = jax.ShapeDtypeStruct((b, s, 2 * HEAD_DIM), F32)
    o, lse = pl.pallas_call(
        functools.partial(_swa_body, dil=dil),
        grid=(b, 2),
        in_specs=[pl.BlockSpec((1, s, HEAD_DIM), lambda i, j: (i, 0, qcol + j)),
                  pl.BlockSpec((1, s, HEAD_DIM), lambda i, j: (i, 0, kcol + j)),
                  pl.BlockSpec((1, s, HEAD_DIM), lambda i, j: (i, 0, vcol + j)),
                  table, table],
        out_specs=[out_spec, out_spec],
        out_shape=[out_sds, out_sds],
        scratch_shapes=[pltpu.VMEM((s, HEAD_DIM), F32)] * 3,
        compiler_params=_params("parallel", "parallel"),
        name=f"swa_group{gi}",
    )(proj, proj, proj, cos_t, sin_t)
    return o.reshape(b * s, 2 * HEAD_DIM), lse.reshape(b * s, 2 * HEAD_DIM)


def _head_rmsnorm(x, g):
    outs = []
    for h in range(N_HEADS):
        sl = slice(h * HEAD_DIM, (h + 1) * HEAD_DIM)
        seg = x[:, sl]
        outs.append(seg * lax.rsqrt(jnp.mean(seg * seg, axis=-1, keepdims=True) + RMS_EPS) * g[:, sl])
    return jnp.concatenate(outs, axis=-1)


def _merge_body(oa_f, oa_b, ob_f, ob_b, z_ref, op_ref, ga_ref, gb_ref,
                o0, o1, o2, l0, l1, l2, y_ref):
    z = z_ref[...].astype(F32)
    y_a = _head_rmsnorm(oa_f[...] + oa_b[...], ga_ref[...]) * (z * jax.nn.sigmoid(z))
    y_b = _head_rmsnorm(jax.nn.sigmoid(op_ref[...].astype(F32)) * (ob_f[...] + ob_b[...]), gb_ref[...])
    lse = [l0[...], l1[...], l2[...]]
    outs = [o0[...], o1[...], o2[...]]
    m = jnp.maximum(jnp.maximum(lse[0], lse[1]), lse[2])
    e = [jnp.exp(l - m) for l in lse]
    inv = 1.0 / (e[0] + e[1] + e[2])
    y_c = jnp.concatenate([e[g] * inv * outs[g] for g in range(3)], axis=-1)
    y_ref[...] = jnp.concatenate([y_a, y_b, y_c], axis=-1).astype(y_ref.dtype)


def _merge(o_a, o_b, proj_a, proj_b, g_a, g_b, swa_o, swa_lse, tm=256):
    m = proj_a.shape[0]
    bw = BRANCH_WIDTH
    row = pl.BlockSpec((tm, bw), lambda i: (i, 0))
    gain = pl.BlockSpec((1, bw), lambda i: (0, 0))
    pair = pl.BlockSpec((tm, 2 * HEAD_DIM), lambda i: (i, 0))
    return pl.pallas_call(
        _merge_body,
        grid=(m // tm,),
        in_specs=[row] * 4
        + [pl.BlockSpec((tm, bw), lambda i: (i, 3)), pl.BlockSpec((tm, bw), lambda i: (i, 3)),
           gain, gain] + [pair] * 6,
        out_specs=pl.BlockSpec((tm, N_BRANCH * bw), lambda i: (i, 0)),
        out_shape=jax.ShapeDtypeStruct((m, N_BRANCH * bw), BF16),
        compiler_params=_params("parallel"),
        name="merge_branches",
    )(*[o.reshape(m, bw) for o in (*o_a, *o_b)], proj_a, proj_b, g_a, g_b, *swa_o, *swa_lse)


def _branch_body(y_ref, w_ref, mp0, mp1, mp2, o_ref):
    acc = None
    for n, mp in enumerate((mp0, mp1, mp2)):
        yn = y_ref[:, n * BRANCH_WIDTH:(n + 1) * BRANCH_WIDTH]
        term = jax.nn.sigmoid(mp[...].astype(F32)) * jnp.dot(yn, w_ref[n], preferred_element_type=F32)
        acc = term if acc is None else acc + term
    o_ref[...] = acc.astype(o_ref.dtype)


def _branch_mix(y, w_branch, merge_pre, tm=512, tn=1024):
    m = y.shape[0]
    nblk = D_MODEL // tn

    def mp_spec(n):
        return pl.BlockSpec((tm, tn), lambda j, i: (i, n * D_MODEL // tn + j))

    return pl.pallas_call(
        _branch_body,
        grid=(nblk, m // tm),
        in_specs=[pl.BlockSpec((tm, N_BRANCH * BRANCH_WIDTH), lambda j, i: (i, 0)),
                  pl.BlockSpec((N_BRANCH, BRANCH_WIDTH, tn), lambda j, i: (0, 0, j)),
                  mp_spec(0), mp_spec(1), mp_spec(2)],
        out_specs=pl.BlockSpec((tm, tn), lambda j, i: (i, j)),
        out_shape=jax.ShapeDtypeStruct((m, D_MODEL), BF16),
        compiler_params=_params("parallel", "parallel"),
        name="branch_mix",
    )(y, w_branch, merge_pre, merge_pre, merge_pre)


def _rms(x, g):
    return x * lax.rsqrt(jnp.mean(x * x, axis=-1, keepdims=True) + RMS_EPS) * g


def _out_proj_body(a_ref, w_ref, x_ref, o_ref):
    o_ref[...] = x_ref[...] + jnp.dot(a_ref[...], w_ref[...], preferred_element_type=F32)


def _out_proj(a, w, x, tm=1024, tn=1024):
    m, k = a.shape
    n = w.shape[1]
    return pl.pallas_call(
        _out_proj_body,
        grid=(n // tn, m // tm),
        in_specs=[pl.BlockSpec((tm, k), lambda j, i: (i, 0)), pl.BlockSpec((k, tn), lambda j, i: (0, j)),
                  pl.BlockSpec((tm, tn), lambda j, i: (i, j))],
        out_specs=pl.BlockSpec((tm, tn), lambda j, i: (i, j)),
        out_shape=jax.ShapeDtypeStruct((m, n), F32),
        compiler_params=_params("parallel", "parallel"),
        name="out_proj",
    )(a, w, x)


def _router_body(x_ref, g_ref, w_ref, h_ref, aff_ref):
    h = _rms(x_ref[...], g_ref[...])
    h_ref[...] = h.astype(h_ref.dtype)
    logits = jnp.dot(h, w_ref[...], preferred_element_type=F32, precision=lax.Precision.HIGHEST)
    e = jnp.exp(logits - jnp.max(logits, axis=-1, keepdims=True))
    aff_ref[...] = e / jnp.sum(e, axis=-1, keepdims=True)


def _router(x, g, w_router, tm=512):
    m, d = x.shape
    return pl.pallas_call(
        _router_body,
        grid=(m // tm,),
        in_specs=[pl.BlockSpec((tm, d), lambda i: (i, 0)), pl.BlockSpec((1, d), lambda i: (0, 0)),
                  pl.BlockSpec((d, N_EXPERTS), lambda i: (0, 0))],
        out_specs=[pl.BlockSpec((tm, d), lambda i: (i, 0)), pl.BlockSpec((tm, N_EXPERTS), lambda i: (i, 0))],
        out_shape=[jax.ShapeDtypeStruct((m, d), BF16), jax.ShapeDtypeStruct((m, N_EXPERTS), F32)],
        compiler_params=_params("parallel"),
        name="moe_router",
    )(x, g.reshape(1, d), w_router)


def _route_body(aff_ref, slot_ref, *, cap):
    aff = aff_ref[0]
    e, s = aff.shape
    thr = jnp.zeros((e, 1), jnp.int32)
    for bit in range(30, -1, -1):
        cand = thr | (1 << bit)
        cnt = jnp.sum(jnp.where(aff >= pltpu.bitcast(cand, F32), 1.0, 0.0), axis=1, keepdims=True)
        thr = jnp.where(cnt >= cap, cand, thr)
    thr_f = pltpu.bitcast(thr, F32)
    gt = aff > thr_f
    eq = aff == thr_f
    n_gt = jnp.sum(jnp.where(gt, 1.0, 0.0), axis=1, keepdims=True)
    upper = (lax.broadcasted_iota(jnp.int32, (s, s), 0)
             <= lax.broadcasted_iota(jnp.int32, (s, s), 1)).astype(BF16)
    eq_rank = jnp.dot(jnp.where(eq, 1.0, 0.0).astype(BF16), upper, preferred_element_type=F32)
    sel = gt | (eq & (eq_rank <= cap - n_gt))
    sel_f = jnp.where(sel, 1.0, 0.0)
    pos = jnp.dot(sel_f.astype(BF16), upper, preferred_element_type=F32)
    slot_ref[0] = jnp.where(sel, pos - 1.0, -1.0).astype(jnp.int32)


def _route(aff_t, cap):
    b, e, s = aff_t.shape
    return pl.pallas_call(
        functools.partial(_route_body, cap=cap),
        grid=(b,),
        in_specs=[pl.BlockSpec((1, e, s), lambda i: (i, 0, 0))],
        out_specs=pl.BlockSpec((1, e, s), lambda i: (i, 0, 0)),
        out_shape=jax.ShapeDtypeStruct((b, e, s), jnp.int32),
        compiler_params=_params("parallel"),
        name="moe_route",
    )(aff_t)


def _gather_body(h_ref, slot_ref, aff_ref, xe_ref, gate_ref, *, cap):
    e = pl.program_id(1)
    slot = slot_ref[0, pl.ds(e, 1), :]
    aff = aff_ref[0, pl.ds(e, 1), :]
    s = slot.shape[1]
    onehot = lax.broadcasted_iota(jnp.int32, (cap, s), 0) == slot
    xe = jnp.dot(jnp.where(onehot, 1.0, 0.0).astype(BF16), h_ref[0], preferred_element_type=F32)
    xe_ref[0, 0] = xe.astype(xe_ref.dtype)
    gate = jnp.sum(jnp.where(onehot, aff, 0.0), axis=1, keepdims=True)
    gate_ref[0, 0] = jnp.broadcast_to(gate, (cap, HEAD_DIM))


def _gather(h, slot_t, aff_t, cap):
    b, s, d = h.shape
    e = slot_t.shape[1]
    return pl.pallas_call(
        functools.partial(_gather_body, cap=cap),
        grid=(b, e),
        in_specs=[pl.BlockSpec((1, s, d), lambda i, j: (i, 0, 0)),
                  pl.BlockSpec((1, e, s), lambda i, j: (i, 0, 0)),
                  pl.BlockSpec((1, e, s), lambda i, j: (i, 0, 0))],
        out_specs=[pl.BlockSpec((1, 1, cap, d), lambda i, j: (j, i, 0, 0)),
                   pl.BlockSpec((1, 1, cap, HEAD_DIM), lambda i, j: (j, i, 0, 0))],
        out_shape=[jax.ShapeDtypeStruct((e, b, cap, d), BF16),
                   jax.ShapeDtypeStruct((e, b, cap, HEAD_DIM), F32)],
        compiler_params=_params("parallel", "arbitrary"),
        name="moe_gather",
    )(h, slot_t, aff_t)


def _expert_body(x_ref, wg_ref, wu_ref, wd_ref, gate_ref, y_ref, hid_ref, *, nf):
    f = pl.program_id(1)
    nb, cap, d = x_ref.shape[1:]
    tf = hid_ref.shape[2]

    @pl.when(f < nf)
    def _():
        x = x_ref[0].reshape(nb * cap, d)
        hg = jnp.dot(x, wg_ref[0, 0].astype(BF16), preferred_element_type=F32)
        hu = jnp.dot(x, wu_ref[0, 0].astype(BF16), preferred_element_type=F32)
        hid_ref[jnp.minimum(f, nf - 1)] = (hg * jax.nn.sigmoid(hg) * hu).astype(BF16)

    @pl.when(f >= nf)
    def _():
        acc = None
        for c in range(nf):
            part = jnp.dot(hid_ref[c], wd_ref[0, 0, c * tf:(c + 1) * tf, :].astype(BF16), preferred_element_type=F32)
            acc = part if acc is None else acc + part
        gate = gate_ref[0].reshape(nb * cap, HEAD_DIM)[:, :1]
        y_ref[:, 0] = (acc * gate).astype(y_ref.dtype).reshape(nb, cap, acc.shape[1])


def _experts(xe, gates, w_gate, w_up, w_down, layer, tf=512, tn=1024):
    e, b, cap, d = xe.shape
    ff = w_gate.shape[3]
    nf, nd = ff // tf, d // tn
    down = lambda f: jnp.maximum(f - nf, 0)
    up_map = lambda i, f: (layer, jnp.where(f < nf, i, jnp.minimum(i + 1, e - 1)), 0, jnp.where(f < nf, f, 0))
    wd_map = lambda i, f: (layer, jnp.where(f == 0, jnp.maximum(i - 1, 0), i), 0, jnp.where(f == 0, nd - 1, down(f)))
    return pl.pallas_call(
        functools.partial(_expert_body, nf=nf),
        grid=(e, nf + nd),
        in_specs=[pl.BlockSpec((1, b, cap, d), lambda i, f: (i, 0, 0, 0)),
                  pl.BlockSpec((1, 1, d, tf), up_map),
                  pl.BlockSpec((1, 1, d, tf), up_map),
                  pl.BlockSpec((1, 1, ff, tn), wd_map),
                  pl.BlockSpec((1, b, cap, HEAD_DIM), lambda i, f: (i, 0, 0, 0))],
        out_specs=pl.BlockSpec((b, 1, cap, tn), lambda i, f: (0, i, 0, down(f))),
        out_shape=jax.ShapeDtypeStruct((b, e, cap, d), BF16),
        scratch_shapes=[pltpu.VMEM((nf, b * cap, tf), BF16)],
        compiler_params=_params("parallel", "arbitrary"),
        name="moe_experts",
    )(xe, w_gate, w_up, w_down, gates)


def _combine_body(x_ref, slot_ref, ye_ref, g_ref, o_ref, *maybe_h_ref, final_norm):
    ne, cap, d = ye_ref.shape[1:]
    slot = slot_ref[0]
    lane = lax.broadcasted_iota(jnp.int32, (slot.shape[0], cap), 1)
    onehot = jnp.concatenate(
        [jnp.where(slot[:, e:e + 1] == lane, 1.0, 0.0).astype(BF16) for e in range(ne)], axis=-1)
    out = x_ref[0] + jnp.dot(onehot, ye_ref[0].reshape(ne * cap, d), preferred_element_type=F32)
    if final_norm:
        o_ref[0] = _rms(out, g_ref[...])
    else:
        o_ref[0] = out
        maybe_h_ref[0][0] = _rms(out, g_ref[...]).astype(BF16)


def _combine(x, slot_nat, ye, norm_g, final_norm, ts=256):
    b, s, d = x.shape
    _, e, cap, _ = ye.shape
    tile = pl.BlockSpec((1, ts, d), lambda i, j: (i, j, 0))
    outs = ([tile], [jax.ShapeDtypeStruct((b, s, d), F32)]) if final_norm else (
        [tile, tile], [jax.ShapeDtypeStruct((b, s, d), F32), jax.ShapeDtypeStruct((b, s, d), BF16)])
    return pl.pallas_call(
        functools.partial(_combine_body, final_norm=final_norm),
        grid=(b, s // ts),
        in_specs=[pl.BlockSpec((1, ts, d), lambda i, j: (i, j, 0)),
                  pl.BlockSpec((1, ts, e), lambda i, j: (i, j, 0)),
                  pl.BlockSpec((1, e, cap, d), lambda i, j: (i, 0, 0, 0)),
                  pl.BlockSpec((1, d), lambda i, j: (0, 0))],
        out_specs=outs[0],
        out_shape=outs[1],
        compiler_params=_params("parallel", "arbitrary"),
        name="moe_combine",
    )(x, slot_nat, ye, norm_g.reshape(1, d))


def _mixer(x2, h, b, s, layer, w_t, conv_w, a_log, dt_bias, gdn_g, i_bias, f_bias, mlstm_g, w_branch, w_out,
           norm2_g, w_router, rope):
    proj_a = _in_proj(h, w_t, layer, SPAN_A)
    proj_b = _in_proj(h, w_t, layer, SPAN_B)
    proj_c = _in_proj(h, w_t, layer, SPAN_C)
    merge_pre = _in_proj(h, w_t, layer, SPAN_MERGE)
    g_nat, g_t = _gate_proj(h, w_t, layer)
    o_a = _gdn_chunks(_gdn_prep(proj_a.reshape(b, s, -1), conv_w), g_nat, g_t, a_log, dt_bias)
    o_b = _mlstm_chunks(proj_b.reshape(b, s, -1), g_nat, g_t, i_bias, f_bias)
    swa = [_swa_group(proj_c.reshape(b, s, -1), rope[0], rope[1], gi) for gi in range(len(SWA_GROUPS))]
    y = _merge(o_a, o_b, proj_a, proj_b, jnp.tile(gdn_g, N_HEADS).reshape(1, BRANCH_WIDTH),
               mlstm_g.reshape(1, BRANCH_WIDTH), [o for o, _ in swa], [l for _, l in swa])
    mixed = _branch_mix(y, w_branch.astype(BF16), merge_pre)
    x1 = _out_proj(mixed, w_out.astype(BF16), x2)
    return (x1, *_router(x1, norm2_g, w_router))


def _moe(x2, h, aff, b, s, w_gate, w_up, w_down, layer, next_g, final_norm):
    cap = EC_CAPACITY_FACTOR * s // N_EXPERTS
    aff_t = jnp.swapaxes(aff.reshape(b, s, N_EXPERTS), 1, 2)
    slot_t = _route(aff_t, cap)
    xe, gates = _gather(h.reshape(b, s, D_MODEL), slot_t, aff_t, cap)
    ye = _experts(xe, gates, w_gate, w_up, w_down, layer)
    outs = _combine(x2.reshape(b, s, D_MODEL), jnp.swapaxes(slot_t, 1, 2), ye, next_g, final_norm)
    return [o.reshape(b * s, D_MODEL) for o in outs]


def kernel(x, norm1_g, w_in, conv_w, gdn_a_log, gdn_dt_bias, gdn_norm_g, mlstm_i_bias, mlstm_f_bias, mlstm_norm_g, w_branch, w_out, norm2_g, w_router, w_gate, w_up, w_down, final_norm_g):
    b, s, d = x.shape
    rope = _rope_tables(s)
    w_t = jnp.swapaxes(w_in, 1, 2)
    x2 = x.reshape(b * s, d)
    h = _rmsnorm(x2, norm1_g[0], BF16)
    for layer in range(DEPTH):
        final = layer == DEPTH - 1
        x2, h2, aff = _mixer(x2, h, b, s, layer, w_t, conv_w[layer], gdn_a_log[layer], gdn_dt_bias[layer],
                             gdn_norm_g[layer], mlstm_i_bias[layer], mlstm_f_bias[layer], mlstm_norm_g[layer],
                             w_branch[layer], w_out[layer], norm2_g[layer], w_router[layer], rope)
        outs = _moe(x2, h2, aff, b, s, w_gate, w_up, w_down, layer,
                    final_norm_g if final else norm1_g[layer + 1], final)
        x2, h = (outs[0], None) if final else outs
    return x2.reshape(b, s, d)
```

```python
import functools

import jax
import jax.numpy as jnp
from jax import lax
from jax.experimental import pallas as pl
from jax.experimental.pallas import tpu as pltpu

F32 = jnp.float32
BF16 = jnp.bfloat16

D_MODEL = 2048
DEPTH = 2
HEAD_DIM = 128
N_HEADS = 6
BRANCH_WIDTH = N_HEADS * HEAD_DIM
CONV_WIDTH = 5
CHUNK = 64
GROUP = 4
ROWS = GROUP * CHUNK
MLSTM_ROWS = 256
SWA_GROUPS = ((128, 1), (512, 4), (2048, 16))
SWA_HALF = 64
SWA_TQ = 128
ROPE_THETA = 500000.0
ROPE_DIM = HEAD_DIM // 4
N_BRANCH = 3
N_EXPERTS = 16
EXPERT_FF = 1024
EC_CAPACITY_FACTOR = 2
RMS_EPS = 1e-6
NEG_INF = -1e30

SPAN_A = (0, 4 * BRANCH_WIDTH)
SPAN_GATE_A = (SPAN_A[0] + SPAN_A[1], 4 * N_HEADS)
SPAN_B = (SPAN_GATE_A[0] + SPAN_GATE_A[1], 4 * BRANCH_WIDTH)
SPAN_GATE_B = (SPAN_B[0] + SPAN_B[1], 4 * N_HEADS)
SPAN_C = (SPAN_GATE_B[0] + SPAN_GATE_B[1], 3 * BRANCH_WIDTH)
SPAN_MERGE = (SPAN_C[0] + SPAN_C[1], N_BRANCH * D_MODEL)
LANES = 128
GATE_LANES = 128
GATE_A = 0
GATE_B = 4 * N_HEADS

VMEM_LIMIT_BYTES = 56 * 1024 * 1024


def _params(*sem):
    return pltpu.CompilerParams(dimension_semantics=sem, vmem_limit_bytes=VMEM_LIMIT_BYTES)


def _dot(a, b):
    return jnp.dot(a.astype(BF16), b.astype(BF16), preferred_element_type=F32)


def _dot_nt(a, b):
    return lax.dot_general(a.astype(BF16), b.astype(BF16), (((1,), (1,)), ((), ())),
                           preferred_element_type=F32)


def _dot_tn(a, b):
    return lax.dot_general(a.astype(BF16), b.astype(BF16), (((0,), (0,)), ((), ())),
                           preferred_element_type=F32)


def _rmsnorm_body(x_ref, g_ref, o_ref):
    x = x_ref[...]
    y = x * lax.rsqrt(jnp.mean(x * x, axis=-1, keepdims=True) + RMS_EPS)
    o_ref[...] = (y * g_ref[...]).astype(o_ref.dtype)


def _rmsnorm(x, g, out_dtype, tm=512):
    m, d = x.shape
    return pl.pallas_call(
        _rmsnorm_body,
        grid=(m // tm,),
        in_specs=[pl.BlockSpec((tm, d), lambda i: (i, 0)), pl.BlockSpec((1, d), lambda i: (0, 0))],
        out_specs=pl.BlockSpec((tm, d), lambda i: (i, 0)),
        out_shape=jax.ShapeDtypeStruct((m, d), out_dtype),
        compiler_params=_params("parallel"),
        name="rmsnorm",
    )(x, g.reshape(1, d))


def _in_proj_body(x_ref, *refs, delta):
    *chunks, o_ref, w_bf = refs
    tn = w_bf.shape[0]

    @pl.when(pl.program_id(1) == 0)
    def _():
        tall = chunks[0].shape[1]
        for c in range(tn // tall):
            if delta:
                piece = jnp.concatenate([chunks[c][0], chunks[c + 1][0]], axis=0)[delta:delta + tall]
            else:
                piece = chunks[c][0]
            w_bf[c * tall:(c + 1) * tall, :] = piece.astype(BF16)

    o_ref[...] = _dot_nt(x_ref[...], w_bf[...]).astype(o_ref.dtype)


def _in_proj(h, w_t, layer, span, tm=2048, tn=BRANCH_WIDTH):
    start, width = span
    m, k = h.shape
    tall = 2 * LANES
    base, delta = divmod(start, tall)
    per_tile = tn // tall
    n_chunks = per_tile + (1 if delta else 0)
    chunk_specs = [pl.BlockSpec((1, tall, k), lambda j, i, c=c: (layer, base + per_tile * j + c, 0))
                   for c in range(n_chunks)]
    return pl.pallas_call(
        functools.partial(_in_proj_body, delta=delta),
        grid=(width // tn, m // tm),
        in_specs=[pl.BlockSpec((tm, k), lambda j, i: (i, 0))] + chunk_specs,
        out_specs=pl.BlockSpec((tm, tn), lambda j, i: (i, j)),
        out_shape=jax.ShapeDtypeStruct((m, width), BF16),
        scratch_shapes=[pltpu.VMEM((tn, k), BF16)],
        compiler_params=_params("parallel", "arbitrary"),
        name="in_proj",
    )(h, *([w_t] * n_chunks))


def _gate_proj_body(x_ref, wa_ref, wb_ref, nat_ref, t_ref):
    row = lax.broadcasted_iota(jnp.int32, wa_ref.shape[1:], 0)
    w = jnp.where(row < GATE_B, wa_ref[0], jnp.where(row < 2 * GATE_B, wb_ref[0], 0.0))
    g = _dot_nt(x_ref[...], w)
    nat_ref[...] = g
    t_ref[...] = g.T


def _gate_proj(h, w_t, layer, tm=512):
    m, k = h.shape
    (chunk_a, row_a), (chunk_b, row_b) = divmod(SPAN_GATE_A[0], LANES), divmod(SPAN_GATE_B[0], LANES)
    assert (row_a, row_b) == (GATE_A, GATE_B) and GATE_LANES == LANES
    return pl.pallas_call(
        _gate_proj_body,
        grid=(m // tm,),
        in_specs=[pl.BlockSpec((tm, k), lambda i: (i, 0)),
                  pl.BlockSpec((1, LANES, k), lambda i: (layer, chunk_a, 0)),
                  pl.BlockSpec((1, LANES, k), lambda i: (layer, chunk_b, 0))],
        out_specs=[pl.BlockSpec((tm, GATE_LANES), lambda i: (i, 0)), pl.BlockSpec((GATE_LANES, tm), lambda i: (0, i))],
        out_shape=[jax.ShapeDtypeStruct((m, GATE_LANES), F32), jax.ShapeDtypeStruct((GATE_LANES, m), F32)],
        compiler_params=_params("parallel"),
        name="gate_proj",
    )(h, w_t, w_t)


def _gdn_prep_body(x_ref, w_ref, o_ref):
    x = x_ref[0].astype(F32)
    w = w_ref[...]
    s = x.shape[0]
    row = lax.broadcasted_iota(jnp.int32, x.shape, 0)
    pad = (CONV_WIDTH - 1) // 2
    acc = x * w[pad:pad + 1]
    for tap in range(CONV_WIDTH):
        off = tap - pad
        if off == 0:
            continue
        shifted = pltpu.roll(x, (-off) % s, 0)
        valid = (row + off >= 0) & (row + off < s)
        acc = acc + jnp.where(valid, shifted, 0.0) * w[tap:tap + 1]
    y = acc * jax.nn.sigmoid(acc)
    r = lax.rsqrt(jnp.sum(y * y, axis=-1, keepdims=True) + RMS_EPS)
    j = pl.program_id(1)
    fac = jnp.where(j < N_HEADS, r * (HEAD_DIM ** -0.5), jnp.where(j < 2 * N_HEADS, r, 1.0))
    o_ref[0] = (y * fac).astype(o_ref.dtype)


def _gdn_prep(proj, conv_w):
    b, s, _ = proj.shape
    nblk = 3 * N_HEADS
    return pl.pallas_call(
        _gdn_prep_body,
        grid=(b, nblk),
        in_specs=[pl.BlockSpec((1, s, HEAD_DIM), lambda i, j: (i, 0, j)),
                  pl.BlockSpec((CONV_WIDTH, HEAD_DIM), lambda i, j: (0, j))],
        out_specs=pl.BlockSpec((1, s, HEAD_DIM), lambda i, j: (i, 0, j)),
        out_shape=jax.ShapeDtypeStruct((b, s, 3 * BRANCH_WIDTH), BF16),
        compiler_params=_params("parallel", "parallel"),
        name="gdn_prep",
    )(proj, conv_w)


def _group_masks(direction, n):
    ii = lax.broadcasted_iota(jnp.int32, (n, n), 0)
    jj = lax.broadcasted_iota(jnp.int32, (n, n), 1)
    shift = CHUNK.bit_length() - 1
    blk = (ii >> shift) == (jj >> shift)
    rel = (ii - jj) if direction == 0 else (jj - ii)
    return blk, blk & (rel >= 0), blk & (rel > 0), blk & (rel <= 0)


def _softplus(x):
    return jnp.maximum(x, 0.0) + jnp.log1p(jnp.exp(-jnp.abs(x)))


def _log_sigmoid(x):
    return -_softplus(-x)


def _bidir_specs(b, s, col0, rows):
    ng = s // rows

    def specs(grp):
        qkv = [pl.BlockSpec((1, rows, BRANCH_WIDTH), lambda i, g, c=c: (i, grp(g), col0 + c)) for c in range(3)]
        return qkv + [pl.BlockSpec((rows, GATE_LANES), lambda i, g: (i * ng + grp(g), 0)),
                      pl.BlockSpec((GATE_LANES, rows), lambda i, g: (0, i * ng + grp(g)))]

    fwd, bwd = (lambda g: g), (lambda g: ng - 1 - g)
    out = [pl.BlockSpec((1, rows, BRANCH_WIDTH), lambda i, g: (i, fwd(g), 0)),
           pl.BlockSpec((1, rows, BRANCH_WIDTH), lambda i, g: (i, bwd(g), 0))]
    return specs(fwd) + specs(bwd), out


def _gdn_body(qf, kf, vf, gnf, gtf, qb, kb, vb, gnb, gtb, al_ref, dt_ref, of, ob, state):
    @pl.when(pl.program_id(1) == 0)
    def _():
        state[...] = jnp.zeros_like(state)

    pair = 2 * CHUNK
    halves = ROWS // pair
    eye = (lax.broadcasted_iota(jnp.int32, (pair, pair), 0)
           == lax.broadcasted_iota(jnp.int32, (pair, pair), 1)).astype(F32)
    lane = lax.broadcasted_iota(jnp.int32, (1, GATE_LANES), 1)
    streams = ((qf, kf, vf, gnf, gtf, of), (qb, kb, vb, gnb, gtb, ob))
    heads = range(N_HEADS)
    sls = [slice(h * HEAD_DIM, (h + 1) * HEAD_DIM) for h in heads]
    items = [(d, h, slice(hf * pair, (hf + 1) * pair)) for d in (0, 1) for h in heads for hf in range(halves)]
    n = range(len(items))
    item_of = lambda d, h, hf: (d * N_HEADS + h) * halves + hf
    masks = [_group_masks(d, pair) for d in (0, 1)]
    blk, incl, strict, incl_t = [[masks[d][m] for d, _, _ in items] for m in range(4)]
    cg = [[GATE_A + d * N_HEADS + h for h in heads] for d in (0, 1)]
    cb = [[c + 2 * N_HEADS for c in cg[d]] for d in (0, 1)]
    g_all, beta_all, gt = [], [], []
    for d, (_, _, _, gn_ref, gt_ref, _) in enumerate(streams):
        gn = gn_ref[...]
        neg_a = sum(jnp.where(lane == cg[d][h], -jnp.exp(al_ref[d:d + 1, h:h + 1]), 0.0) for h in heads)
        dtb = sum(jnp.where(lane == cg[d][h], dt_ref[d:d + 1, h:h + 1], 0.0) for h in heads)
        g_all.append(neg_a * _softplus(gn + dtb))
        beta_all.append(jax.nn.sigmoid(gn))
        gt.append(gt_ref[...])
    g_c = [g_all[d][rw, cg[d][h]:cg[d][h] + 1] for d, h, rw in items]
    g_r = [-jnp.exp(al_ref[d:d + 1, h:h + 1]) * _softplus(gt[d][cg[d][h]:cg[d][h] + 1, rw] + dt_ref[d:d + 1, h:h + 1])
           for d, h, rw in items]
    beta = [beta_all[d][rw, cb[d][h]:cb[d][h] + 1] for d, h, rw in items]
    gc_c = [jnp.sum(jnp.where(incl[i], g_r[i], 0.0), axis=1, keepdims=True) for i in n]
    gc_r = [jnp.sum(jnp.where(incl_t[i], g_c[i], 0.0), axis=0, keepdims=True) for i in n]
    gtot = [jnp.sum(jnp.where(blk[i], g_r[i], 0.0), axis=1, keepdims=True) for i in n]
    decay = [jnp.where(incl[i], jnp.exp(jnp.where(incl[i], gc_c[i] - gc_r[i], 0.0)), 0.0) for i in n]
    q = [streams[d][0][0, rw, sls[h]].astype(F32) for d, h, rw in items]
    k = [streams[d][1][0, rw, sls[h]].astype(F32) for d, h, rw in items]
    v = [streams[d][2][0, rw, sls[h]].astype(F32) for d, h, rw in items]
    gram = [_dot_nt(jnp.concatenate([k[i], q[i]], axis=0), k[i]) for i in n]
    p = [-jnp.where(strict[i], beta[i] * decay[i] * gram[i][:pair], 0.0) for i in n]
    t_inv = [eye + p[i] for i in n]
    p = [_dot(p[i], p[i]) for i in n]
    for _ in range(4):
        both = [_dot(p[i], jnp.concatenate([t_inv[i], p[i]], axis=1)) for i in n]
        t_inv = [t_inv[i] + both[i][:, :pair] for i in n]
        p = [both[i][:, pair:] for i in n]
    t_inv = [t_inv[i] + _dot(p[i], t_inv[i]) for i in n]
    e_c = [jnp.exp(gc_c[i]) for i in n]
    sol = [_dot(t_inv[i], jnp.concatenate([v[i] * beta[i], k[i] * (beta[i] * e_c[i])], axis=-1)) for i in n]
    w_v = [sol[i][:, :HEAD_DIM] for i in n]
    w_k = [sol[i][:, HEAD_DIM:] for i in n]
    qkd = [gram[i][pair:] * decay[i] for i in n]
    q_dec = [q[i] * e_c[i] for i in n]
    k_dec = [k[i] * jnp.exp(gtot[i] - gc_c[i]) for i in n]
    c_dec = [jnp.exp(gtot[i]) for i in n]
    dh = [(d, h) for d in (0, 1) for h in heads]
    st = [state[d, h] for d, h in dh]
    for step in range(GROUP):
        chunk = (step, GROUP - 1 - step)
        it, lr, r = [], [], []
        for d, h in dh:
            hf, lc = divmod(chunk[d], pair // CHUNK)
            it.append(item_of(d, h, hf))
            lr.append(slice(lc * CHUNK, (lc + 1) * CHUNK))
            r.append(slice(chunk[d] * CHUNK, (chunk[d] + 1) * CHUNK))
        m = range(len(dh))
        xs = [_dot(jnp.concatenate([w_k[it[j]][lr[j]], q_dec[it[j]][lr[j]]], axis=0), st[j]) for j in m]
        u = [w_v[it[j]][lr[j]] - xs[j][:CHUNK] for j in m]
        for j, (d, h) in enumerate(dh):
            streams[d][5][0, r[j], sls[h]] = xs[j][CHUNK:] + _dot(qkd[it[j]][lr[j], lr[j]], u[j])
        st = [c_dec[it[j]][lr[j].start:lr[j].start + 1] * st[j] + _dot_tn(k_dec[it[j]][lr[j]], u[j]) for j in m]
    for j, (d, h) in enumerate(dh):
        state[d, h] = st[j]


def _gdn_chunks(qkv, g_nat, g_t, a_log, dt_bias):
    b, s, _ = qkv.shape
    in_specs, out_specs = _bidir_specs(b, s, 0, ROWS)
    small = pl.BlockSpec((2, N_HEADS), lambda i, g: (0, 0))
    sds = jax.ShapeDtypeStruct((b, s, BRANCH_WIDTH), F32)
    ins = [qkv, qkv, qkv, g_nat, g_t]
    return pl.pallas_call(
        _gdn_body,
        grid=(b, s // ROWS),
        in_specs=in_specs + [small, small],
        out_specs=out_specs,
        out_shape=[sds, sds],
        scratch_shapes=[pltpu.VMEM((2, N_HEADS, HEAD_DIM, HEAD_DIM), F32)],
        compiler_params=_params("parallel", "arbitrary"),
        name="gdn_chunks",
    )(*ins, *ins, a_log, dt_bias)


def _split_bf16(x):
    hi = x.astype(BF16)
    return hi, (x - hi.astype(F32)).astype(BF16)


def _mlstm_body(qf, kf, vf, gnf, gtf, qb, kb, vb, gnb, gtb, ib_ref, fb_ref, of, ob, cn_state, m_state):
    @pl.when(pl.program_id(1) == 0)
    def _():
        cn_state[...] = jnp.zeros_like(cn_state)
        m_state[...] = jnp.zeros_like(m_state)

    rows = MLSTM_ROWS
    ii = lax.broadcasted_iota(jnp.int32, (rows, rows), 0)
    jj = lax.broadcasted_iota(jnp.int32, (rows, rows), 1)
    ones = jnp.ones((rows, HEAD_DIM), F32)
    lane = lax.broadcasted_iota(jnp.int32, (1, GATE_LANES), 1)
    heads = range(N_HEADS)
    sls = [slice(h * HEAD_DIM, (h + 1) * HEAD_DIM) for h in heads]
    streams = ((qf, kf, vf, gnf, gtf, of), (qb, kb, vb, gnb, gtb, ob))
    for d, (q_ref, k_ref, v_ref, gn_ref, gt_ref, o_ref) in enumerate(streams):
        incl = (jj <= ii) if d == 0 else (jj >= ii)
        incl_t = (ii <= jj) if d == 0 else (ii >= jj)
        gn = gn_ref[...]
        gt = gt_ref[...]
        ci = [GATE_B + d * N_HEADS + h for h in heads]
        cf = [c + 2 * N_HEADS for c in ci]
        ib = [ib_ref[d:d + 1, h:h + 1] for h in heads]
        fb = [fb_ref[d:d + 1, h:h + 1] for h in heads]
        q = [q_ref[0, :, sl].astype(F32) for sl in sls]
        v_ext = [jnp.concatenate([v_ref[0, :, sl].astype(F32), ones], axis=1) for sl in sls]
        k = [k_ref[0, :, sl].astype(F32) * (HEAD_DIM ** -0.5) for sl in sls]
        qk = [_dot_nt(q[h], k[h]) for h in heads]
        cn = [cn_state[d, h] for h in heads]
        m0 = [m_state[d, h][:, :1] for h in heads]
        qcn = [_dot(q[h], cn[h]) for h in heads]
        i_all = gn + sum(jnp.where(lane == ci[h], ib[h], 0.0) for h in heads)
        f_all = _log_sigmoid(gn + sum(jnp.where(lane == cf[h], fb[h], 0.0) for h in heads))
        bc2 = jnp.dot(incl.astype(BF16), jnp.concatenate(_split_bf16(f_all), axis=1), preferred_element_type=F32)
        b_all = bc2[:, :GATE_LANES] + bc2[:, GATE_LANES:]
        btot_all = jnp.sum(f_all, axis=0, keepdims=True)
        i_r = [gt[ci[h]:ci[h] + 1, :] + ib[h] for h in heads]
        f_r = [_log_sigmoid(gt[cf[h]:cf[h] + 1, :] + fb[h]) for h in heads]
        f_rows = jnp.concatenate(f_r + [jnp.zeros((8 - N_HEADS, rows), F32)], axis=0)
        br2 = jnp.dot(jnp.concatenate(_split_bf16(f_rows), axis=0), incl_t.astype(BF16), preferred_element_type=F32)
        b_rows = br2[:8] + br2[8:]
        i_c = [i_all[:, ci[h]:ci[h] + 1] for h in heads]
        b_c = [b_all[:, cf[h]:cf[h] + 1] for h in heads]
        btot = [btot_all[:, cf[h]:cf[h] + 1] for h in heads]
        w_r = [i_r[h] - b_rows[h:h + 1] for h in heads]
        mx = [jnp.maximum(m0[h], jnp.max(jnp.where(incl, w_r[h], NEG_INF), axis=1, keepdims=True)) for h in heads]
        dmat = [jnp.exp(jnp.where(incl, w_r[h] - mx[h], NEG_INF)) * qk[h] for h in heads]
        dv = [_dot(dmat[h], v_ext[h]) for h in heads]
        lend = [btot[h] - b_c[h] + i_c[h] for h in heads]
        m_new = [jnp.maximum(btot[h] + m0[h], jnp.max(lend[h], axis=0, keepdims=True)) for h in heads]
        scale = [jnp.exp(btot[h] + m0[h] - m_new[h]) for h in heads]
        kw = [k[h] * jnp.exp(lend[h] - m_new[h]) for h in heads]
        kv = [_dot_tn(kw[h], v_ext[h]) for h in heads]
        for h in heads:
            inter = jnp.exp(m0[h] - mx[h])
            den = inter * qcn[h][:, HEAD_DIM:HEAD_DIM + 1] + dv[h][:, HEAD_DIM:HEAD_DIM + 1]
            inv = 1.0 / jnp.maximum(jnp.abs(den), jnp.exp(-(b_c[h] + mx[h])))
            o_ref[0, :, sls[h]] = (inter * qcn[h][:, :HEAD_DIM] + dv[h][:, :HEAD_DIM]) * inv
            cn_state[d, h] = scale[h] * cn[h] + kv[h]
            m_state[d, h] = jnp.broadcast_to(m_new[h], (1, HEAD_DIM))


def _mlstm_chunks(proj, g_nat, g_t, i_bias, f_bias):
    b, s, _ = proj.shape
    in_specs, out_specs = _bidir_specs(b, s, 0, MLSTM_ROWS)
    small = pl.BlockSpec((2, N_HEADS), lambda i, g: (0, 0))
    sds = jax.ShapeDtypeStruct((b, s, BRANCH_WIDTH), F32)
    ins = [proj, proj, proj, g_nat, g_t]
    return pl.pallas_call(
        _mlstm_body,
        grid=(b, s // MLSTM_ROWS),
        in_specs=in_specs + [small, small],
        out_specs=out_specs,
        out_shape=[sds, sds],
        scratch_shapes=[pltpu.VMEM((2, N_HEADS, HEAD_DIM, 2 * HEAD_DIM), F32),
                        pltpu.VMEM((2, N_HEADS, 1, HEAD_DIM), F32)],
        compiler_params=_params("parallel", "arbitrary"),
        name="mlstm_chunks",
    )(*ins, *ins, i_bias, f_bias)


def _rope_tables(s):
    half = ROPE_DIM // 2
    inv_freq = jnp.power(ROPE_THETA, -jnp.arange(half, dtype=F32) * (2.0 / ROPE_DIM))
    ang = jnp.arange(s, dtype=F32)[:, None] * inv_freq[None, :]
    cos, sin = jnp.cos(ang), jnp.sin(ang)
    rest = HEAD_DIM - ROPE_DIM
    cos_t = jnp.concatenate([cos, cos, jnp.ones((s, rest), F32)], axis=-1)
    sin_t = jnp.concatenate([-sin, sin, jnp.zeros((s, rest), F32)], axis=-1)
    return cos_t, sin_t


def _rotary(x, cos_t, sin_t):
    half = ROPE_DIM // 2
    lane = lax.broadcasted_iota(jnp.int32, x.shape, 1)
    swapped = jnp.where(lane < half, pltpu.roll(x, HEAD_DIM - half, 1), pltpu.roll(x, half, 1))
    return x * cos_t + swapped * sin_t


def _swa_body(q_ref, k_ref, v_ref, cos_ref, sin_ref, o_ref, lse_ref, q_rot, k_rot, v_f32, *, dil):
    s_len = q_ref.shape[1]
    t = s_len // dil
    win = min(t, SWA_TQ + 2 * SWA_HALF)
    cos_t, sin_t = cos_ref[...], sin_ref[...]
    q_rot[...] = _rotary(q_ref[0].astype(F32), cos_t, sin_t) * (HEAD_DIM ** -0.5)
    k_rot[...] = _rotary(k_ref[0].astype(F32), cos_t, sin_t)
    v_f32[...] = v_ref[0].astype(F32)

    def rows(first, count):
        return pl.ds(first, count, stride=dil) if dil > 1 else pl.ds(first, count)

    blocks = [(r, tb) for r in range(dil) for tb in range(t // SWA_TQ)]
    batch = 4
    for b0 in range(0, len(blocks), batch):
        grp = blocks[b0:b0 + batch]
        starts = [min(max(tb * SWA_TQ - SWA_HALF, 0), t - win) for _, tb in grp]
        q_rows = [rows(r + tb * SWA_TQ * dil, SWA_TQ) for r, tb in grp]
        kv_rows = [rows(r + st * dil, win) for (r, _), st in zip(grp, starts)]
        sc = [_dot_nt(q_rot[qr, :], k_rot[kr, :]) for qr, kr in zip(q_rows, kv_rows)]
        ms, ps, dens = [], [], []
        for (_, tb), st, s in zip(grp, starts, sc):
            rel = (tb * SWA_TQ - st) + (lax.broadcasted_iota(jnp.int32, s.shape, 0)
                                        - lax.broadcasted_iota(jnp.int32, s.shape, 1))
            s = jnp.where(jnp.abs(rel) <= SWA_HALF, s, NEG_INF)
            m = jnp.max(s, axis=1, keepdims=True)
            p = jnp.exp(s - m)
            ms.append(m)
            ps.append(p)
            dens.append(jnp.sum(p, axis=1, keepdims=True))
        pv = [_dot(p, v_f32[kr, :]) for p, kr in zip(ps, kv_rows)]
        for qr, o, m, den in zip(q_rows, pv, ms, dens):
            o_ref[0, qr, :] = o / den
            lse_ref[0, qr, :] = jnp.broadcast_to(m + jnp.log(den), (SWA_TQ, HEAD_DIM))


def _swa_group(proj, cos_t, sin_t, gi):
    b, s, _ = proj.shape
    dil = SWA_GROUPS[gi][1]
    qcol = 2 * gi
    kcol, vcol = qcol + N_HEADS, qcol + 2 * N_HEADS
    table = pl.BlockSpec((s, HEAD_DIM), lambda i, j: (0, 0))
    out_spec = pl.BlockSpec((1, s, HEAD_DIM), lambda i, j: (i, 0, j))
    out_sds = jax.ShapeDtypeStruct((b, s, 2 * HEAD_DIM), F32)
    o, lse = pl.pallas_call(
        functools.partial(_swa_body, dil=dil),
        grid=(b, 2),
        in_specs=[pl.BlockSpec((1, s, HEAD_DIM), lambda i, j: (i, 0, qcol + j)),
                  pl.BlockSpec((1, s, HEAD_DIM), lambda i, j: (i, 0, kcol + j)),
                  pl.BlockSpec((1, s, HEAD_DIM), lambda i, j: (i, 0, vcol + j)),
                  table, table],
        out_specs=[out_spec, out_spec],
        out_shape=[out_sds, out_sds],
        scratch_shapes=[pltpu.VMEM((s, HEAD_DIM), F32)] * 3,
        compiler_params=_params("parallel", "parallel"),
        name=f"swa_group{gi}",
    )(proj, proj, proj, cos_t, sin_t)
    return o.reshape(b * s, 2 * HEAD_DIM), lse.reshape(b * s, 2 * HEAD_DIM)


def _head_rmsnorm(x, g):
    outs = []
    for h in range(N_HEADS):
        sl = slice(h * HEAD_DIM, (h + 1) * HEAD_DIM)
        seg = x[:, sl]
        outs.append(seg * lax.rsqrt(jnp.mean(seg * seg, axis=-1, keepdims=True) + RMS_EPS) * g[:, sl])
    return jnp.concatenate(outs, axis=-1)


def _merge_body(oa_f, oa_b, ob_f, ob_b, z_ref, op_ref, ga_ref, gb_ref,
                o0, o1, o2, l0, l1, l2, y_ref):
    z = z_ref[...].astype(F32)
    y_a = _head_rmsnorm(oa_f[...] + oa_b[...], ga_ref[...]) * (z * jax.nn.sigmoid(z))
    y_b = _head_rmsnorm(jax.nn.sigmoid(op_ref[...].astype(F32)) * (ob_f[...] + ob_b[...]), gb_ref[...])
    lse = [l0[...], l1[...], l2[...]]
    outs = [o0[...], o1[...], o2[...]]
    m = jnp.maximum(jnp.maximum(lse[0], lse[1]), lse[2])
    e = [jnp.exp(l - m) for l in lse]
    inv = 1.0 / (e[0] + e[1] + e[2])
    y_c = jnp.concatenate([e[g] * inv * outs[g] for g in range(3)], axis=-1)
    y_ref[...] = jnp.concatenate([y_a, y_b, y_c], axis=-1).astype(y_ref.dtype)


def _merge(o_a, o_b, proj_a, proj_b, g_a, g_b, swa_o, swa_lse, tm=256):
    m = proj_a.shape[0]
    bw = BRANCH_WIDTH
    row = pl.BlockSpec((tm, bw), lambda i: (i, 0))
    gain = pl.BlockSpec((1, bw), lambda i: (0, 0))
    pair = pl.BlockSpec((tm, 2 * HEAD_DIM), lambda i: (i, 0))
    return pl.pallas_call(
        _merge_body,
        grid=(m // tm,),
        in_specs=[row] * 4
        + [pl.BlockSpec((tm, bw), lambda i: (i, 3)), pl.BlockSpec((tm, bw), lambda i: (i, 3)),
           gain, gain] + [pair] * 6,
        out_specs=pl.BlockSpec((tm, N_BRANCH * bw), lambda i: (i, 0)),
        out_shape=jax.ShapeDtypeStruct((m, N_BRANCH * bw), BF16),
        compiler_params=_params("parallel"),
        name="merge_branches",
    )(*[o.reshape(m, bw) for o in (*o_a, *o_b)], proj_a, proj_b, g_a, g_b, *swa_o, *swa_lse)


def _branch_body(y_ref, w_ref, mp0, mp1, mp2, o_ref):
    acc = None
    for n, mp in enumerate((mp0, mp1, mp2)):
        yn = y_ref[:, n * BRANCH_WIDTH:(n + 1) * BRANCH_WIDTH]
        term = jax.nn.sigmoid(mp[...].astype(F32)) * jnp.dot(yn, w_ref[n], preferred_element_type=F32)
        acc = term if acc is None else acc + term
    o_ref[...] = acc.astype(o_ref.dtype)


def _branch_mix(y, w_branch, merge_pre, tm=512, tn=1024):
    m = y.shape[0]
    nblk = D_MODEL // tn

    def mp_spec(n):
        return pl.BlockSpec((tm, tn), lambda j, i: (i, n * D_MODEL // tn + j))

    return pl.pallas_call(
        _branch_body,
        grid=(nblk, m // tm),
        in_specs=[pl.BlockSpec((tm, N_BRANCH * BRANCH_WIDTH), lambda j, i: (i, 0)),
                  pl.BlockSpec((N_BRANCH, BRANCH_WIDTH, tn), lambda j, i: (0, 0, j)),
                  mp_spec(0), mp_spec(1), mp_spec(2)],
        out_specs=pl.BlockSpec((tm, tn), lambda j, i: (i, j)),
        out_shape=jax.ShapeDtypeStruct((m, D_MODEL), BF16),
        compiler_params=_params("parallel", "parallel"),
        name="branch_mix",
    )(y, w_branch, merge_pre, merge_pre, merge_pre)


def _rms(x, g):
    return x * lax.rsqrt(jnp.mean(x * x, axis=-1, keepdims=True) + RMS_EPS) * g


def _out_proj_body(a_ref, w_ref, x_ref, o_ref):
    o_ref[...] = x_ref[...] + jnp.dot(a_ref[...], w_ref[...], preferred_element_type=F32)


def _out_proj(a, w, x, tm=1024, tn=1024):
    m, k = a.shape
    n = w.shape[1]
    return pl.pallas_call(
        _out_proj_body,
        grid=(n // tn, m // tm),
        in_specs=[pl.BlockSpec((tm, k), lambda j, i: (i, 0)), pl.BlockSpec((k, tn), lambda j, i: (0, j)),
                  pl.BlockSpec((tm, tn), lambda j, i: (i, j))],
        out_specs=pl.BlockSpec((tm, tn), lambda j, i: (i, j)),
        out_shape=jax.ShapeDtypeStruct((m, n), F32),
        compiler_params=_params("parallel", "parallel"),
        name="out_proj",
    )(a, w, x)


def _router_body(x_ref, g_ref, w_ref, h_ref, aff_ref):
    h = _rms(x_ref[...], g_ref[...])
    h_hi, h_lo = _split_bf16(h)
    w_hi, w_lo = _split_bf16(w_ref[...])
    h_ref[...] = h_hi
    n_e = w_hi.shape[1]
    both = jnp.dot(h_hi, jnp.concatenate([w_hi, w_lo], axis=1), preferred_element_type=F32)
    logits = both[:, :n_e] + both[:, n_e:] + jnp.dot(h_lo, w_hi, preferred_element_type=F32)
    e = jnp.exp(logits - jnp.max(logits, axis=-1, keepdims=True))
    aff_ref[...] = e / jnp.sum(e, axis=-1, keepdims=True)


def _router(x, g, w_router, tm=512):
    m, d = x.shape
    return pl.pallas_call(
        _router_body,
        grid=(m // tm,),
        in_specs=[pl.BlockSpec((tm, d), lambda i: (i, 0)), pl.BlockSpec((1, d), lambda i: (0, 0)),
                  pl.BlockSpec((d, N_EXPERTS), lambda i: (0, 0))],
        out_specs=[pl.BlockSpec((tm, d), lambda i: (i, 0)), pl.BlockSpec((tm, N_EXPERTS), lambda i: (i, 0))],
        out_shape=[jax.ShapeDtypeStruct((m, d), BF16), jax.ShapeDtypeStruct((m, N_EXPERTS), F32)],
        compiler_params=_params("parallel"),
        name="moe_router",
    )(x, g.reshape(1, d), w_router)


def _route_body(aff_ref, slot_ref, *, cap):
    aff = aff_ref[0]
    e, s = aff.shape
    thr = jnp.zeros((e, 1), jnp.int32)
    for bit in range(30, -1, -1):
        cand = thr | (1 << bit)
        cnt = jnp.sum(jnp.where(aff >= pltpu.bitcast(cand, F32), 1.0, 0.0), axis=1, keepdims=True)
        thr = jnp.where(cnt >= cap, cand, thr)
    thr_f = pltpu.bitcast(thr, F32)
    gt = aff > thr_f
    eq = aff == thr_f
    n_gt = jnp.sum(jnp.where(gt, 1.0, 0.0), axis=1, keepdims=True)
    upper = (lax.broadcasted_iota(jnp.int32, (s, s), 0)
             <= lax.broadcasted_iota(jnp.int32, (s, s), 1)).astype(BF16)
    eq_rank = jnp.dot(jnp.where(eq, 1.0, 0.0).astype(BF16), upper, preferred_element_type=F32)
    sel = gt | (eq & (eq_rank <= cap - n_gt))
    sel_f = jnp.where(sel, 1.0, 0.0)
    pos = jnp.dot(sel_f.astype(BF16), upper, preferred_element_type=F32)
    slot_ref[0] = jnp.where(sel, pos - 1.0, -1.0).astype(jnp.int32)


def _route(aff_t, cap):
    b, e, s = aff_t.shape
    return pl.pallas_call(
        functools.partial(_route_body, cap=cap),
        grid=(b,),
        in_specs=[pl.BlockSpec((1, e, s), lambda i: (i, 0, 0))],
        out_specs=pl.BlockSpec((1, e, s), lambda i: (i, 0, 0)),
        out_shape=jax.ShapeDtypeStruct((b, e, s), jnp.int32),
        compiler_params=_params("parallel"),
        name="moe_route",
    )(aff_t)


def _gather_body(h_ref, slot_ref, aff_ref, xe_ref, gate_ref, *, cap):
    e = pl.program_id(1)
    slot = slot_ref[0, pl.ds(e, 1), :]
    aff = aff_ref[0, pl.ds(e, 1), :]
    s = slot.shape[1]
    onehot = lax.broadcasted_iota(jnp.int32, (cap, s), 0) == slot
    xe = jnp.dot(jnp.where(onehot, 1.0, 0.0).astype(BF16), h_ref[0], preferred_element_type=F32)
    xe_ref[0, 0] = xe.astype(xe_ref.dtype)
    gate = jnp.sum(jnp.where(onehot, aff, 0.0), axis=1, keepdims=True)
    gate_ref[0, 0] = jnp.broadcast_to(gate, (cap, HEAD_DIM))


def _gather(h, slot_t, aff_t, cap):
    b, s, d = h.shape
    e = slot_t.shape[1]
    return pl.pallas_call(
        functools.partial(_gather_body, cap=cap),
        grid=(b, e),
        in_specs=[pl.BlockSpec((1, s, d), lambda i, j: (i, 0, 0)),
                  pl.BlockSpec((1, e, s), lambda i, j: (i, 0, 0)),
                  pl.BlockSpec((1, e, s), lambda i, j: (i, 0, 0))],
        out_specs=[pl.BlockSpec((1, 1, cap, d), lambda i, j: (j, i, 0, 0)),
                   pl.BlockSpec((1, 1, cap, HEAD_DIM), lambda i, j: (j, i, 0, 0))],
        out_shape=[jax.ShapeDtypeStruct((e, b, cap, d), BF16),
                   jax.ShapeDtypeStruct((e, b, cap, HEAD_DIM), F32)],
        compiler_params=_params("parallel", "arbitrary"),
        name="moe_gather",
    )(h, slot_t, aff_t)


def _expert_body(x_ref, wg_ref, wu_ref, wd_ref, gate_ref, y_ref, hid_ref, *, nf):
    f = pl.program_id(1)
    nb, cap, d = x_ref.shape[1:]
    tf = hid_ref.shape[2]

    @pl.when(f < nf)
    def _():
        x = x_ref[0].reshape(nb * cap, d)
        hg = jnp.dot(x, wg_ref[0, 0].astype(BF16), preferred_element_type=F32)
        hu = jnp.dot(x, wu_ref[0, 0].astype(BF16), preferred_element_type=F32)
        hid_ref[jnp.minimum(f, nf - 1)] = (hg * jax.nn.sigmoid(hg) * hu).astype(BF16)

    @pl.when(f >= nf)
    def _():
        acc = None
        for c in range(nf):
            part = jnp.dot(hid_ref[c], wd_ref[0, 0, c * tf:(c + 1) * tf, :].astype(BF16), preferred_element_type=F32)
            acc = part if acc is None else acc + part
        gate = gate_ref[0].reshape(nb * cap, HEAD_DIM)[:, :1]
        y_ref[:, 0] = (acc * gate).astype(y_ref.dtype).reshape(nb, cap, acc.shape[1])


def _experts(xe, gates, w_gate, w_up, w_down, layer, tf=512, tn=1024):
    e, b, cap, d = xe.shape
    ff = w_gate.shape[3]
    nf, nd = ff // tf, d // tn
    down = lambda f: jnp.maximum(f - nf, 0)
    up_map = lambda i, f: (layer, jnp.where(f < nf, i, jnp.minimum(i + 1, e - 1)), 0, jnp.where(f < nf, f, 0))
    wd_map = lambda i, f: (layer, jnp.where(f == 0, jnp.maximum(i - 1, 0), i), 0, jnp.where(f == 0, nd - 1, down(f)))
    return pl.pallas_call(
        functools.partial(_expert_body, nf=nf),
        grid=(e, nf + nd),
        in_specs=[pl.BlockSpec((1, b, cap, d), lambda i, f: (i, 0, 0, 0)),
                  pl.BlockSpec((1, 1, d, tf), up_map),
                  pl.BlockSpec((1, 1, d, tf), up_map),
                  pl.BlockSpec((1, 1, ff, tn), wd_map),
                  pl.BlockSpec((1, b, cap, HEAD_DIM), lambda i, f: (i, 0, 0, 0))],
        out_specs=pl.BlockSpec((b, 1, cap, tn), lambda i, f: (0, i, 0, down(f))),
        out_shape=jax.ShapeDtypeStruct((b, e, cap, d), BF16),
        scratch_shapes=[pltpu.VMEM((nf, b * cap, tf), BF16)],
        compiler_params=_params("parallel", "arbitrary"),
        name="moe_experts",
    )(xe, w_gate, w_up, w_down, gates)


def _combine_body(x_ref, slot_ref, ye_ref, g_ref, o_ref, *maybe_h_ref, final_norm):
    ne, cap, d = ye_ref.shape[1:]
    slot = slot_ref[0]
    lane = lax.broadcasted_iota(jnp.int32, (slot.shape[0], cap), 1)
    onehot = jnp.concatenate(
        [jnp.where(slot[:, e:e + 1] == lane, 1.0, 0.0).astype(BF16) for e in range(ne)], axis=-1)
    out = x_ref[0] + jnp.dot(onehot, ye_ref[0].reshape(ne * cap, d), preferred_element_type=F32)
    if final_norm:
        o_ref[0] = _rms(out, g_ref[...])
    else:
        o_ref[0] = out
        maybe_h_ref[0][0] = _rms(out, g_ref[...]).astype(BF16)


def _combine(x, slot_nat, ye, norm_g, final_norm, ts=256):
    b, s, d = x.shape
    _, e, cap, _ = ye.shape
    tile = pl.BlockSpec((1, ts, d), lambda i, j: (i, j, 0))
    outs = ([tile], [jax.ShapeDtypeStruct((b, s, d), F32)]) if final_norm else (
        [tile, tile], [jax.ShapeDtypeStruct((b, s, d), F32), jax.ShapeDtypeStruct((b, s, d), BF16)])
    return pl.pallas_call(
        functools.partial(_combine_body, final_norm=final_norm),
        grid=(b, s // ts),
        in_specs=[pl.BlockSpec((1, ts, d), lambda i, j: (i, j, 0)),
                  pl.BlockSpec((1, ts, e), lambda i, j: (i, j, 0)),
                  pl.BlockSpec((1, e, cap, d), lambda i, j: (i, 0, 0, 0)),
                  pl.BlockSpec((1, d), lambda i, j: (0, 0))],
        out_specs=outs[0],
        out_shape=outs[1],
        compiler_params=_params("parallel", "arbitrary"),
        name="moe_combine",
    )(x, slot_nat, ye, norm_g.reshape(1, d))


def _mixer(x2, h, b, s, layer, w_t, conv_w, a_log, dt_bias, gdn_g, i_bias, f_bias, mlstm_g, w_branch, w_out,
           norm2_g, w_router, rope):
    proj_a = _in_proj(h, w_t, layer, SPAN_A)
    proj_b = _in_proj(h, w_t, layer, SPAN_B)
    proj_c = _in_proj(h, w_t, layer, SPAN_C)
    merge_pre = _in_proj(h, w_t, layer, SPAN_MERGE)
    g_nat, g_t = _gate_proj(h, w_t, layer)
    o_a = _gdn_chunks(_gdn_prep(proj_a.reshape(b, s, -1), conv_w), g_nat, g_t, a_log, dt_bias)
    o_b = _mlstm_chunks(proj_b.reshape(b, s, -1), g_nat, g_t, i_bias, f_bias)
    swa = [_swa_group(proj_c.reshape(b, s, -1), rope[0], rope[1], gi) for gi in range(len(SWA_GROUPS))]
    y = _merge(o_a, o_b, proj_a, proj_b, jnp.tile(gdn_g, N_HEADS).reshape(1, BRANCH_WIDTH),
               mlstm_g.reshape(1, BRANCH_WIDTH), [o for o, _ in swa], [l for _, l in swa])
    mixed = _branch_mix(y, w_branch.astype(BF16), merge_pre)
    x1 = _out_proj(mixed, w_out.astype(BF16), x2)
    return (x1, *_router(x1, norm2_g, w_router))


def _moe(x2, h, aff, b, s, w_gate, w_up, w_down, layer, next_g, final_norm):
    cap = EC_CAPACITY_FACTOR * s // N_EXPERTS
    aff_t = jnp.swapaxes(aff.reshape(b, s, N_EXPERTS), 1, 2)
    slot_t = _route(aff_t, cap)
    xe, gates = _gather(h.reshape(b, s, D_MODEL), slot_t, aff_t, cap)
    ye = _experts(xe, gates, w_gate, w_up, w_down, layer)
    outs = _combine(x2.reshape(b, s, D_MODEL), jnp.swapaxes(slot_t, 1, 2), ye, next_g, final_norm)
    return [o.reshape(b * s, D_MODEL) for o in outs]


def kernel(x, norm1_g, w_in, conv_w, gdn_a_log, gdn_dt_bias, gdn_norm_g, mlstm_i_bias, mlstm_f_bias, mlstm_norm_g, w_branch, w_out, norm2_g, w_router, w_gate, w_up, w_down, final_norm_g):
    b, s, d = x.shape
    rope = _rope_tables(s)
    w_t = jnp.swapaxes(w_in, 1, 2)
    x2 = x.reshape(b * s, d)
    h = _rmsnorm(x2, norm1_g[0], BF16)
    for layer in range(DEPTH):
        final = layer == DEPTH - 1
        x2, h2, aff = _mixer(x2, h, b, s, layer, w_t, conv_w[layer], gdn_a_log[layer], gdn_dt_bias[layer],
                             gdn_norm_g[layer], mlstm_i_bias[layer], mlstm_f_bias[layer], mlstm_norm_g[layer],
                             w_branch[layer], w_out[layer], norm2_g[layer], w_router[layer], rope)
        outs = _moe(x2, h2, aff, b, s, w_gate, w_up, w_down, layer,
                    final_norm_g if final else norm1_g[layer + 1], final)
        x2, h = (outs[0], None) if final else outs
    return x2.reshape(b, s, d)
```

```python
import functools

import jax
import jax.numpy as jnp
from jax import lax
from jax.experimental import pallas as pl
from jax.experimental.pallas import tpu as pltpu

F32 = jnp.float32
BF16 = jnp.bfloat16

D_MODEL = 2048
DEPTH = 2
HEAD_DIM = 128
N_HEADS = 6
BRANCH_WIDTH = N_HEADS * HEAD_DIM
CONV_WIDTH = 5
CHUNK = 64
GROUP = 4
ROWS = GROUP * CHUNK
MLSTM_ROWS = 512
SWA_GROUPS = ((128, 1), (512, 4), (2048, 16))
SWA_HALF = 64
SWA_TQ = 128
ROPE_THETA = 500000.0
ROPE_DIM = HEAD_DIM // 4
N_BRANCH = 3
N_EXPERTS = 16
EXPERT_FF = 1024
EC_CAPACITY_FACTOR = 2
RMS_EPS = 1e-6
NEG_INF = -1e30

SPAN_A = (0, 4 * BRANCH_WIDTH)
SPAN_GATE_A = (SPAN_A[0] + SPAN_A[1], 4 * N_HEADS)
SPAN_B = (SPAN_GATE_A[0] + SPAN_GATE_A[1], 4 * BRANCH_WIDTH)
SPAN_GATE_B = (SPAN_B[0] + SPAN_B[1], 4 * N_HEADS)
SPAN_C = (SPAN_GATE_B[0] + SPAN_GATE_B[1], 3 * BRANCH_WIDTH)
SPAN_MERGE = (SPAN_C[0] + SPAN_C[1], N_BRANCH * D_MODEL)
LANES = 128
GATE_LANES = 128
GATE_A = 0
GATE_B = 4 * N_HEADS

VMEM_LIMIT_BYTES = 56 * 1024 * 1024


def _params(*sem):
    return pltpu.CompilerParams(dimension_semantics=sem, vmem_limit_bytes=VMEM_LIMIT_BYTES)


def _dot(a, b):
    return jnp.dot(a.astype(BF16), b.astype(BF16), preferred_element_type=F32)


def _dot_nt(a, b):
    return lax.dot_general(a.astype(BF16), b.astype(BF16), (((1,), (1,)), ((), ())),
                           preferred_element_type=F32)


def _dot_tn(a, b):
    return lax.dot_general(a.astype(BF16), b.astype(BF16), (((0,), (0,)), ((), ())),
                           preferred_element_type=F32)


def _rmsnorm_body(x_ref, g_ref, o_ref):
    x = x_ref[...]
    y = x * lax.rsqrt(jnp.mean(x * x, axis=-1, keepdims=True) + RMS_EPS)
    o_ref[...] = (y * g_ref[...]).astype(o_ref.dtype)


def _rmsnorm(x, g, out_dtype, tm=512):
    m, d = x.shape
    return pl.pallas_call(
        _rmsnorm_body,
        grid=(m // tm,),
        in_specs=[pl.BlockSpec((tm, d), lambda i: (i, 0)), pl.BlockSpec((1, d), lambda i: (0, 0))],
        out_specs=pl.BlockSpec((tm, d), lambda i: (i, 0)),
        out_shape=jax.ShapeDtypeStruct((m, d), out_dtype),
        compiler_params=_params("parallel"),
        name="rmsnorm",
    )(x, g.reshape(1, d))


def _in_proj_body(x_ref, *refs, delta):
    *chunks, o_ref, w_bf = refs
    tn = w_bf.shape[0]

    @pl.when(pl.program_id(1) == 0)
    def _():
        tall = chunks[0].shape[1]
        for c in range(tn // tall):
            if delta:
                piece = jnp.concatenate([chunks[c][0], chunks[c + 1][0]], axis=0)[delta:delta + tall]
            else:
                piece = chunks[c][0]
            w_bf[c * tall:(c + 1) * tall, :] = piece.astype(BF16)

    o_ref[...] = _dot_nt(x_ref[...], w_bf[...]).astype(o_ref.dtype)


def _in_proj(h, w_t, layer, span, tm=2048, tn=BRANCH_WIDTH):
    start, width = span
    m, k = h.shape
    tall = 2 * LANES
    base, delta = divmod(start, tall)
    per_tile = tn // tall
    n_chunks = per_tile + (1 if delta else 0)
    chunk_specs = [pl.BlockSpec((1, tall, k), lambda j, i, c=c: (layer, base + per_tile * j + c, 0))
                   for c in range(n_chunks)]
    return pl.pallas_call(
        functools.partial(_in_proj_body, delta=delta),
        grid=(width // tn, m // tm),
        in_specs=[pl.BlockSpec((tm, k), lambda j, i: (i, 0))] + chunk_specs,
        out_specs=pl.BlockSpec((tm, tn), lambda j, i: (i, j)),
        out_shape=jax.ShapeDtypeStruct((m, width), BF16),
        scratch_shapes=[pltpu.VMEM((tn, k), BF16)],
        compiler_params=_params("parallel", "arbitrary"),
        name="in_proj",
    )(h, *([w_t] * n_chunks))


def _gate_proj_body(x_ref, wa_ref, wb_ref, nat_ref, t_ref):
    row = lax.broadcasted_iota(jnp.int32, wa_ref.shape[1:], 0)
    w = jnp.where(row < GATE_B, wa_ref[0], jnp.where(row < 2 * GATE_B, wb_ref[0], 0.0))
    g = _dot_nt(x_ref[...], w)
    nat_ref[...] = g
    t_ref[...] = g.T


def _gate_proj(h, w_t, layer, tm=512):
    m, k = h.shape
    (chunk_a, row_a), (chunk_b, row_b) = divmod(SPAN_GATE_A[0], LANES), divmod(SPAN_GATE_B[0], LANES)
    assert (row_a, row_b) == (GATE_A, GATE_B) and GATE_LANES == LANES
    return pl.pallas_call(
        _gate_proj_body,
        grid=(m // tm,),
        in_specs=[pl.BlockSpec((tm, k), lambda i: (i, 0)),
                  pl.BlockSpec((1, LANES, k), lambda i: (layer, chunk_a, 0)),
                  pl.BlockSpec((1, LANES, k), lambda i: (layer, chunk_b, 0))],
        out_specs=[pl.BlockSpec((tm, GATE_LANES), lambda i: (i, 0)), pl.BlockSpec((GATE_LANES, tm), lambda i: (0, i))],
        out_shape=[jax.ShapeDtypeStruct((m, GATE_LANES), F32), jax.ShapeDtypeStruct((GATE_LANES, m), F32)],
        compiler_params=_params("parallel"),
        name="gate_proj",
    )(h, w_t, w_t)


def _gdn_prep_body(x_ref, w_ref, o_ref):
    x = x_ref[0].astype(F32)
    w = w_ref[...]
    s = x.shape[0]
    row = lax.broadcasted_iota(jnp.int32, x.shape, 0)
    pad = (CONV_WIDTH - 1) // 2
    acc = x * w[pad:pad + 1]
    for tap in range(CONV_WIDTH):
        off = tap - pad
        if off == 0:
            continue
        shifted = pltpu.roll(x, (-off) % s, 0)
        valid = (row + off >= 0) & (row + off < s)
        acc = acc + jnp.where(valid, shifted, 0.0) * w[tap:tap + 1]
    y = acc * jax.nn.sigmoid(acc)
    r = lax.rsqrt(jnp.sum(y * y, axis=-1, keepdims=True) + RMS_EPS)
    j = pl.program_id(1)
    fac = jnp.where(j < N_HEADS, r * (HEAD_DIM ** -0.5), jnp.where(j < 2 * N_HEADS, r, 1.0))
    o_ref[0] = (y * fac).astype(o_ref.dtype)


def _gdn_prep(proj, conv_w):
    b, s, _ = proj.shape
    nblk = 3 * N_HEADS
    return pl.pallas_call(
        _gdn_prep_body,
        grid=(b, nblk),
        in_specs=[pl.BlockSpec((1, s, HEAD_DIM), lambda i, j: (i, 0, j)),
                  pl.BlockSpec((CONV_WIDTH, HEAD_DIM), lambda i, j: (0, j))],
        out_specs=pl.BlockSpec((1, s, HEAD_DIM), lambda i, j: (i, 0, j)),
        out_shape=jax.ShapeDtypeStruct((b, s, 3 * BRANCH_WIDTH), BF16),
        compiler_params=_params("parallel", "parallel"),
        name="gdn_prep",
    )(proj, conv_w)


def _group_masks(direction, n):
    ii = lax.broadcasted_iota(jnp.int32, (n, n), 0)
    jj = lax.broadcasted_iota(jnp.int32, (n, n), 1)
    shift = CHUNK.bit_length() - 1
    blk = (ii >> shift) == (jj >> shift)
    rel = (ii - jj) if direction == 0 else (jj - ii)
    return blk, blk & (rel >= 0), blk & (rel > 0), blk & (rel <= 0)


def _softplus(x):
    return jnp.maximum(x, 0.0) + jnp.log1p(jnp.exp(-jnp.abs(x)))


def _log_sigmoid(x):
    return -_softplus(-x)


def _bidir_specs(b, s, col0, rows):
    ng = s // rows

    def specs(grp):
        qkv = [pl.BlockSpec((1, rows, BRANCH_WIDTH), lambda i, g, c=c: (i, grp(g), col0 + c)) for c in range(3)]
        return qkv + [pl.BlockSpec((rows, GATE_LANES), lambda i, g: (i * ng + grp(g), 0)),
                      pl.BlockSpec((GATE_LANES, rows), lambda i, g: (0, i * ng + grp(g)))]

    fwd, bwd = (lambda g: g), (lambda g: ng - 1 - g)
    out = [pl.BlockSpec((1, rows, BRANCH_WIDTH), lambda i, g: (i, fwd(g), 0)),
           pl.BlockSpec((1, rows, BRANCH_WIDTH), lambda i, g: (i, bwd(g), 0))]
    return specs(fwd) + specs(bwd), out


def _gdn_body(qf, kf, vf, gnf, gtf, qb, kb, vb, gnb, gtb, al_ref, dt_ref, of, ob, state):
    @pl.when(pl.program_id(1) == 0)
    def _():
        state[...] = jnp.zeros_like(state)

    pair = 2 * CHUNK
    halves = ROWS // pair
    eye = (lax.broadcasted_iota(jnp.int32, (pair, pair), 0)
           == lax.broadcasted_iota(jnp.int32, (pair, pair), 1)).astype(F32)
    lane = lax.broadcasted_iota(jnp.int32, (1, GATE_LANES), 1)
    streams = ((qf, kf, vf, gnf, gtf, of), (qb, kb, vb, gnb, gtb, ob))
    heads = range(N_HEADS)
    sls = [slice(h * HEAD_DIM, (h + 1) * HEAD_DIM) for h in heads]
    items = [(d, h, slice(hf * pair, (hf + 1) * pair)) for d in (0, 1) for h in heads for hf in range(halves)]
    n = range(len(items))
    item_of = lambda d, h, hf: (d * N_HEADS + h) * halves + hf
    masks = [_group_masks(d, pair) for d in (0, 1)]
    blk, incl, strict, incl_t = [[masks[d][m] for d, _, _ in items] for m in range(4)]
    cg = [[GATE_A + d * N_HEADS + h for h in heads] for d in (0, 1)]
    cb = [[c + 2 * N_HEADS for c in cg[d]] for d in (0, 1)]
    g_all, beta_all, gt = [], [], []
    for d, (_, _, _, gn_ref, gt_ref, _) in enumerate(streams):
        gn = gn_ref[...]
        neg_a = sum(jnp.where(lane == cg[d][h], -jnp.exp(al_ref[d:d + 1, h:h + 1]), 0.0) for h in heads)
        dtb = sum(jnp.where(lane == cg[d][h], dt_ref[d:d + 1, h:h + 1], 0.0) for h in heads)
        g_all.append(neg_a * _softplus(gn + dtb))
        beta_all.append(jax.nn.sigmoid(gn))
        gt.append(gt_ref[...])
    g_c = [g_all[d][rw, cg[d][h]:cg[d][h] + 1] for d, h, rw in items]
    g_r = [-jnp.exp(al_ref[d:d + 1, h:h + 1]) * _softplus(gt[d][cg[d][h]:cg[d][h] + 1, rw] + dt_ref[d:d + 1, h:h + 1])
           for d, h, rw in items]
    beta = [beta_all[d][rw, cb[d][h]:cb[d][h] + 1] for d, h, rw in items]
    gc_c = [jnp.sum(jnp.where(incl[i], g_r[i], 0.0), axis=1, keepdims=True) for i in n]
    gc_r = [jnp.sum(jnp.where(incl_t[i], g_c[i], 0.0), axis=0, keepdims=True) for i in n]
    gtot = [jnp.sum(jnp.where(blk[i], g_r[i], 0.0), axis=1, keepdims=True) for i in n]
    decay = [jnp.where(incl[i], jnp.exp(jnp.where(incl[i], gc_c[i] - gc_r[i], 0.0)), 0.0) for i in n]
    q = [streams[d][0][0, rw, sls[h]].astype(F32) for d, h, rw in items]
    k = [streams[d][1][0, rw, sls[h]].astype(F32) for d, h, rw in items]
    v = [streams[d][2][0, rw, sls[h]].astype(F32) for d, h, rw in items]
    gram = [_dot_nt(jnp.concatenate([k[i], q[i]], axis=0), k[i]) for i in n]
    p = [-jnp.where(strict[i], beta[i] * decay[i] * gram[i][:pair], 0.0) for i in n]
    t_inv = [eye + p[i] for i in n]
    p = [_dot(p[i], p[i]) for i in n]
    for _ in range(4):
        both = [_dot(p[i], jnp.concatenate([t_inv[i], p[i]], axis=1)) for i in n]
        t_inv = [t_inv[i] + both[i][:, :pair] for i in n]
        p = [both[i][:, pair:] for i in n]
    t_inv = [t_inv[i] + _dot(p[i], t_inv[i]) for i in n]
    e_c = [jnp.exp(gc_c[i]) for i in n]
    sol = [_dot(t_inv[i], jnp.concatenate([v[i] * beta[i], k[i] * (beta[i] * e_c[i])], axis=-1)) for i in n]
    w_v = [sol[i][:, :HEAD_DIM] for i in n]
    w_k = [sol[i][:, HEAD_DIM:] for i in n]
    qkd = [gram[i][pair:] * decay[i] for i in n]
    q_dec = [q[i] * e_c[i] for i in n]
    k_dec = [k[i] * jnp.exp(gtot[i] - gc_c[i]) for i in n]
    c_dec = [jnp.exp(gtot[i]) for i in n]
    dh = [(d, h) for d in (0, 1) for h in heads]
    st = [state[d, h] for d, h in dh]
    for step in range(GROUP):
        chunk = (step, GROUP - 1 - step)
        it, lr, r = [], [], []
        for d, h in dh:
            hf, lc = divmod(chunk[d], pair // CHUNK)
            it.append(item_of(d, h, hf))
            lr.append(slice(lc * CHUNK, (lc + 1) * CHUNK))
            r.append(slice(chunk[d] * CHUNK, (chunk[d] + 1) * CHUNK))
        m = range(len(dh))
        xs = [_dot(jnp.concatenate([w_k[it[j]][lr[j]], q_dec[it[j]][lr[j]]], axis=0), st[j]) for j in m]
        u = [w_v[it[j]][lr[j]] - xs[j][:CHUNK] for j in m]
        for j, (d, h) in enumerate(dh):
            streams[d][5][0, r[j], sls[h]] = (xs[j][CHUNK:] + _dot(qkd[it[j]][lr[j], lr[j]], u[j])).astype(BF16)
        st = [c_dec[it[j]][lr[j].start:lr[j].start + 1] * st[j] + _dot_tn(k_dec[it[j]][lr[j]], u[j]) for j in m]
    for j, (d, h) in enumerate(dh):
        state[d, h] = st[j]


def _gdn_chunks(qkv, g_nat, g_t, a_log, dt_bias):
    b, s, _ = qkv.shape
    in_specs, out_specs = _bidir_specs(b, s, 0, ROWS)
    small = pl.BlockSpec((2, N_HEADS), lambda i, g: (0, 0))
    sds = jax.ShapeDtypeStruct((b, s, BRANCH_WIDTH), BF16)
    ins = [qkv, qkv, qkv, g_nat, g_t]
    return pl.pallas_call(
        _gdn_body,
        grid=(b, s // ROWS),
        in_specs=in_specs + [small, small],
        out_specs=out_specs,
        out_shape=[sds, sds],
        scratch_shapes=[pltpu.VMEM((2, N_HEADS, HEAD_DIM, HEAD_DIM), F32)],
        compiler_params=_params("parallel", "arbitrary"),
        name="gdn_chunks",
    )(*ins, *ins, a_log, dt_bias)


def _split_bf16(x):
    hi = x.astype(BF16)
    return hi, (x - hi.astype(F32)).astype(BF16)


def _mlstm_body(qf, kf, vf, gnf, gtf, qb, kb, vb, gnb, gtb, ib_ref, fb_ref, of, ob, cn_state, m_state):
    @pl.when(pl.program_id(1) == 0)
    def _():
        cn_state[...] = jnp.zeros_like(cn_state)
        m_state[...] = jnp.zeros_like(m_state)

    rows = MLSTM_ROWS
    ii = lax.broadcasted_iota(jnp.int32, (rows, rows), 0)
    jj = lax.broadcasted_iota(jnp.int32, (rows, rows), 1)
    ones = jnp.ones((rows, HEAD_DIM), F32)
    lane = lax.broadcasted_iota(jnp.int32, (1, GATE_LANES), 1)
    heads = range(N_HEADS)
    sls = [slice(h * HEAD_DIM, (h + 1) * HEAD_DIM) for h in heads]
    streams = ((qf, kf, vf, gnf, gtf, of), (qb, kb, vb, gnb, gtb, ob))
    for d, (q_ref, k_ref, v_ref, gn_ref, gt_ref, o_ref) in enumerate(streams):
        incl = (jj <= ii) if d == 0 else (jj >= ii)
        incl_t = (ii <= jj) if d == 0 else (ii >= jj)
        gn = gn_ref[...]
        gt = gt_ref[...]
        ci = [GATE_B + d * N_HEADS + h for h in heads]
        cf = [c + 2 * N_HEADS for c in ci]
        ib = [ib_ref[d:d + 1, h:h + 1] for h in heads]
        fb = [fb_ref[d:d + 1, h:h + 1] for h in heads]
        q = [q_ref[0, :, sl].astype(F32) for sl in sls]
        v_ext = [jnp.concatenate([v_ref[0, :, sl].astype(F32), ones], axis=1) for sl in sls]
        k = [k_ref[0, :, sl].astype(F32) * (HEAD_DIM ** -0.5) for sl in sls]
        qk = [_dot_nt(q[h], k[h]) for h in heads]
        cn = [cn_state[d, h] for h in heads]
        m0 = [m_state[d, h][:, :1] for h in heads]
        qcn = [_dot(q[h], cn[h]) for h in heads]
        i_all = gn + sum(jnp.where(lane == ci[h], ib[h], 0.0) for h in heads)
        f_all = _log_sigmoid(gn + sum(jnp.where(lane == cf[h], fb[h], 0.0) for h in heads))
        bc2 = jnp.dot(incl.astype(BF16), jnp.concatenate(_split_bf16(f_all), axis=1), preferred_element_type=F32)
        b_all = bc2[:, :GATE_LANES] + bc2[:, GATE_LANES:]
        btot_all = jnp.sum(f_all, axis=0, keepdims=True)
        i_r = [gt[ci[h]:ci[h] + 1, :] + ib[h] for h in heads]
        f_r = [_log_sigmoid(gt[cf[h]:cf[h] + 1, :] + fb[h]) for h in heads]
        f_rows = jnp.concatenate(f_r + [jnp.zeros((8 - N_HEADS, rows), F32)], axis=0)
        br2 = jnp.dot(jnp.concatenate(_split_bf16(f_rows), axis=0), incl_t.astype(BF16), preferred_element_type=F32)
        b_rows = br2[:8] + br2[8:]
        i_c = [i_all[:, ci[h]:ci[h] + 1] for h in heads]
        b_c = [b_all[:, cf[h]:cf[h] + 1] for h in heads]
        btot = [btot_all[:, cf[h]:cf[h] + 1] for h in heads]
        w_r = [i_r[h] - b_rows[h:h + 1] for h in heads]
        mx = [jnp.maximum(m0[h], jnp.max(jnp.where(incl, w_r[h], NEG_INF), axis=1, keepdims=True)) for h in heads]
        dmat = [jnp.exp(jnp.where(incl, w_r[h] - mx[h], NEG_INF)) * qk[h] for h in heads]
        dv = [_dot(dmat[h], v_ext[h]) for h in heads]
        lend = [btot[h] - b_c[h] + i_c[h] for h in heads]
        m_new = [jnp.maximum(btot[h] + m0[h], jnp.max(lend[h], axis=0, keepdims=True)) for h in heads]
        scale = [jnp.exp(btot[h] + m0[h] - m_new[h]) for h in heads]
        kw = [k[h] * jnp.exp(lend[h] - m_new[h]) for h in heads]
        kv = [_dot_tn(kw[h], v_ext[h]) for h in heads]
        for h in heads:
            inter = jnp.exp(m0[h] - mx[h])
            den = inter * qcn[h][:, HEAD_DIM:HEAD_DIM + 1] + dv[h][:, HEAD_DIM:HEAD_DIM + 1]
            inv = 1.0 / jnp.maximum(jnp.abs(den), jnp.exp(-(b_c[h] + mx[h])))
            o_ref[0, :, sls[h]] = ((inter * qcn[h][:, :HEAD_DIM] + dv[h][:, :HEAD_DIM]) * inv).astype(BF16)
            cn_state[d, h] = scale[h] * cn[h] + kv[h]
            m_state[d, h] = jnp.broadcast_to(m_new[h], (1, HEAD_DIM))


def _mlstm_chunks(proj, g_nat, g_t, i_bias, f_bias):
    b, s, _ = proj.shape
    in_specs, out_specs = _bidir_specs(b, s, 0, MLSTM_ROWS)
    small = pl.BlockSpec((2, N_HEADS), lambda i, g: (0, 0))
    sds = jax.ShapeDtypeStruct((b, s, BRANCH_WIDTH), BF16)
    ins = [proj, proj, proj, g_nat, g_t]
    return pl.pallas_call(
        _mlstm_body,
        grid=(b, s // MLSTM_ROWS),
        in_specs=in_specs + [small, small],
        out_specs=out_specs,
        out_shape=[sds, sds],
        scratch_shapes=[pltpu.VMEM((2, N_HEADS, HEAD_DIM, 2 * HEAD_DIM), F32),
                        pltpu.VMEM((2, N_HEADS, 1, HEAD_DIM), F32)],
        compiler_params=_params("parallel", "arbitrary"),
        name="mlstm_chunks",
    )(*ins, *ins, i_bias, f_bias)


def _rope_tables(s):
    half = ROPE_DIM // 2
    inv_freq = jnp.power(ROPE_THETA, -jnp.arange(half, dtype=F32) * (2.0 / ROPE_DIM))
    ang = jnp.arange(s, dtype=F32)[:, None] * inv_freq[None, :]
    cos, sin = jnp.cos(ang), jnp.sin(ang)
    rest = HEAD_DIM - ROPE_DIM
    cos_t = jnp.concatenate([cos, cos, jnp.ones((s, rest), F32)], axis=-1)
    sin_t = jnp.concatenate([-sin, sin, jnp.zeros((s, rest), F32)], axis=-1)
    return cos_t, sin_t


def _rotary(x, cos_t, sin_t):
    half = ROPE_DIM // 2
    lane = lax.broadcasted_iota(jnp.int32, x.shape, 1)
    swapped = jnp.where(lane < half, pltpu.roll(x, HEAD_DIM - half, 1), pltpu.roll(x, half, 1))
    return x * cos_t + swapped * sin_t


def _swa_body(q_ref, k_ref, v_ref, cos_ref, sin_ref, o_ref, lse_ref, q_rot, k_rot, v_f32, *, dil):
    s_len = q_ref.shape[1]
    t = s_len // dil
    win = min(t, SWA_TQ + 2 * SWA_HALF)
    cos_t, sin_t = cos_ref[...], sin_ref[...]
    q_rot[...] = _rotary(q_ref[0].astype(F32), cos_t, sin_t) * (HEAD_DIM ** -0.5)
    k_rot[...] = _rotary(k_ref[0].astype(F32), cos_t, sin_t)
    v_f32[...] = v_ref[0].astype(F32)

    def rows(first, count):
        return pl.ds(first, count, stride=dil) if dil > 1 else pl.ds(first, count)

    blocks = [(r, tb) for r in range(dil) for tb in range(t // SWA_TQ)]
    batch = 4
    for b0 in range(0, len(blocks), batch):
        grp = blocks[b0:b0 + batch]
        starts = [min(max(tb * SWA_TQ - SWA_HALF, 0), t - win) for _, tb in grp]
        q_rows = [rows(r + tb * SWA_TQ * dil, SWA_TQ) for r, tb in grp]
        kv_rows = [rows(r + st * dil, win) for (r, _), st in zip(grp, starts)]
        sc = [_dot_nt(q_rot[qr, :], k_rot[kr, :]) for qr, kr in zip(q_rows, kv_rows)]
        ms, ps, dens = [], [], []
        for (_, tb), st, s in zip(grp, starts, sc):
            rel = (tb * SWA_TQ - st) + (lax.broadcasted_iota(jnp.int32, s.shape, 0)
                                        - lax.broadcasted_iota(jnp.int32, s.shape, 1))
            s = jnp.where(jnp.abs(rel) <= SWA_HALF, s, NEG_INF)
            m = jnp.max(s, axis=1, keepdims=True)
            p = jnp.exp(s - m)
            ms.append(m)
            ps.append(p)
            dens.append(jnp.sum(p, axis=1, keepdims=True))
        pv = [_dot(p, v_f32[kr, :]) for p, kr in zip(ps, kv_rows)]
        for qr, o, m, den in zip(q_rows, pv, ms, dens):
            o_ref[0, qr, :] = o / den
            lse_ref[0, qr, :] = jnp.broadcast_to(m + jnp.log(den), (SWA_TQ, HEAD_DIM))


def _swa_group(proj, cos_t, sin_t, gi):
    b, s, _ = proj.shape
    dil = SWA_GROUPS[gi][1]
    qcol = 2 * gi
    kcol, vcol = qcol + N_HEADS, qcol + 2 * N_HEADS
    table = pl.BlockSpec((s, HEAD_DIM), lambda i, j: (0, 0))
    out_spec = pl.BlockSpec((1, s, HEAD_DIM), lambda i, j: (i, 0, j))
    out_sds = jax.ShapeDtypeStruct((b, s, 2 * HEAD_DIM), F32)
    o, lse = pl.pallas_call(
        functools.partial(_swa_body, dil=dil),
        grid=(b, 2),
        in_specs=[pl.BlockSpec((1, s, HEAD_DIM), lambda i, j: (i, 0, qcol + j)),
                  pl.BlockSpec((1, s, HEAD_DIM), lambda i, j: (i, 0, kcol + j)),
                  pl.BlockSpec((1, s, HEAD_DIM), lambda i, j: (i, 0, vcol + j)),
                  table, table],
        out_specs=[out_spec, out_spec],
        out_shape=[out_sds, out_sds],
        scratch_shapes=[pltpu.VMEM((s, HEAD_DIM), F32)] * 3,
        compiler_params=_params("parallel", "parallel"),
        name=f"swa_group{gi}",
    )(proj, proj, proj, cos_t, sin_t)
    return o.reshape(b * s, 2 * HEAD_DIM), lse.reshape(b * s, 2 * HEAD_DIM)


def _head_rmsnorm(x, g):
    outs = []
    for h in range(N_HEADS):
        sl = slice(h * HEAD_DIM, (h + 1) * HEAD_DIM)
        seg = x[:, sl]
        outs.append(seg * lax.rsqrt(jnp.mean(seg * seg, axis=-1, keepdims=True) + RMS_EPS) * g[:, sl])
    return jnp.concatenate(outs, axis=-1)


def _merge_body(oa_f, oa_b, ob_f, ob_b, z_ref, op_ref, ga_ref, gb_ref,
                o0, o1, o2, l0, l1, l2, y_ref):
    z = z_ref[...].astype(F32)
    o_a = oa_f[...].astype(F32) + oa_b[...].astype(F32)
    o_b = ob_f[...].astype(F32) + ob_b[...].astype(F32)
    y_a = _head_rmsnorm(o_a, ga_ref[...]) * (z * jax.nn.sigmoid(z))
    y_b = _head_rmsnorm(jax.nn.sigmoid(op_ref[...].astype(F32)) * o_b, gb_ref[...])
    lse = [l0[...], l1[...], l2[...]]
    outs = [o0[...], o1[...], o2[...]]
    m = jnp.maximum(jnp.maximum(lse[0], lse[1]), lse[2])
    e = [jnp.exp(l - m) for l in lse]
    inv = 1.0 / (e[0] + e[1] + e[2])
    y_c = jnp.concatenate([e[g] * inv * outs[g] for g in range(3)], axis=-1)
    y_ref[...] = jnp.concatenate([y_a, y_b, y_c], axis=-1).astype(y_ref.dtype)


def _merge(o_a, o_b, proj_a, proj_b, g_a, g_b, swa_o, swa_lse, tm=256):
    m = proj_a.shape[0]
    bw = BRANCH_WIDTH
    row = pl.BlockSpec((tm, bw), lambda i: (i, 0))
    gain = pl.BlockSpec((1, bw), lambda i: (0, 0))
    pair = pl.BlockSpec((tm, 2 * HEAD_DIM), lambda i: (i, 0))
    return pl.pallas_call(
        _merge_body,
        grid=(m // tm,),
        in_specs=[row] * 4
        + [pl.BlockSpec((tm, bw), lambda i: (i, 3)), pl.BlockSpec((tm, bw), lambda i: (i, 3)),
           gain, gain] + [pair] * 6,
        out_specs=pl.BlockSpec((tm, N_BRANCH * bw), lambda i: (i, 0)),
        out_shape=jax.ShapeDtypeStruct((m, N_BRANCH * bw), BF16),
        compiler_params=_params("parallel"),
        name="merge_branches",
    )(*[o.reshape(m, bw) for o in (*o_a, *o_b)], proj_a, proj_b, g_a, g_b, *swa_o, *swa_lse)


def _branch_body(y_ref, w_ref, mp0, mp1, mp2, o_ref):
    acc = None
    for n, mp in enumerate((mp0, mp1, mp2)):
        yn = y_ref[:, n * BRANCH_WIDTH:(n + 1) * BRANCH_WIDTH]
        term = jax.nn.sigmoid(mp[...].astype(F32)) * jnp.dot(yn, w_ref[n], preferred_element_type=F32)
        acc = term if acc is None else acc + term
    o_ref[...] = acc.astype(o_ref.dtype)


def _branch_mix(y, w_branch, merge_pre, tm=512, tn=1024):
    m = y.shape[0]
    nblk = D_MODEL // tn

    def mp_spec(n):
        return pl.BlockSpec((tm, tn), lambda j, i: (i, n * D_MODEL // tn + j))

    return pl.pallas_call(
        _branch_body,
        grid=(nblk, m // tm),
        in_specs=[pl.BlockSpec((tm, N_BRANCH * BRANCH_WIDTH), lambda j, i: (i, 0)),
                  pl.BlockSpec((N_BRANCH, BRANCH_WIDTH, tn), lambda j, i: (0, 0, j)),
                  mp_spec(0), mp_spec(1), mp_spec(2)],
        out_specs=pl.BlockSpec((tm, tn), lambda j, i: (i, j)),
        out_shape=jax.ShapeDtypeStruct((m, D_MODEL), BF16),
        compiler_params=_params("parallel", "parallel"),
        name="branch_mix",
    )(y, w_branch, merge_pre, merge_pre, merge_pre)


def _rms(x, g):
    return x * lax.rsqrt(jnp.mean(x * x, axis=-1, keepdims=True) + RMS_EPS) * g


def _out_proj_body(a_ref, w_ref, x_ref, o_ref):
    o_ref[...] = x_ref[...] + jnp.dot(a_ref[...], w_ref[...], preferred_element_type=F32)


def _out_proj(a, w, x, tm=1024, tn=1024):
    m, k = a.shape
    n = w.shape[1]
    return pl.pallas_call(
        _out_proj_body,
        grid=(n // tn, m // tm),
        in_specs=[pl.BlockSpec((tm, k), lambda j, i: (i, 0)), pl.BlockSpec((k, tn), lambda j, i: (0, j)),
                  pl.BlockSpec((tm, tn), lambda j, i: (i, j))],
        out_specs=pl.BlockSpec((tm, tn), lambda j, i: (i, j)),
        out_shape=jax.ShapeDtypeStruct((m, n), F32),
        compiler_params=_params("parallel", "parallel"),
        name="out_proj",
    )(a, w, x)


def _router_body(x_ref, g_ref, w_ref, h_ref, aff_ref):
    h = _rms(x_ref[...], g_ref[...])
    h_hi, h_lo = _split_bf16(h)
    w_hi, w_lo = _split_bf16(w_ref[...])
    h_ref[...] = h_hi
    n_e = w_hi.shape[1]
    both = jnp.dot(h_hi, jnp.concatenate([w_hi, w_lo], axis=1), preferred_element_type=F32)
    logits = both[:, :n_e] + both[:, n_e:] + jnp.dot(h_lo, w_hi, preferred_element_type=F32)
    e = jnp.exp(logits - jnp.max(logits, axis=-1, keepdims=True))
    aff_ref[...] = e / jnp.sum(e, axis=-1, keepdims=True)


def _router(x, g, w_router, tm=512):
    m, d = x.shape
    return pl.pallas_call(
        _router_body,
        grid=(m // tm,),
        in_specs=[pl.BlockSpec((tm, d), lambda i: (i, 0)), pl.BlockSpec((1, d), lambda i: (0, 0)),
                  pl.BlockSpec((d, N_EXPERTS), lambda i: (0, 0))],
        out_specs=[pl.BlockSpec((tm, d), lambda i: (i, 0)), pl.BlockSpec((tm, N_EXPERTS), lambda i: (i, 0))],
        out_shape=[jax.ShapeDtypeStruct((m, d), BF16), jax.ShapeDtypeStruct((m, N_EXPERTS), F32)],
        compiler_params=_params("parallel"),
        name="moe_router",
    )(x, g.reshape(1, d), w_router)


def _route_body(aff_ref, slot_ref, *, cap):
    aff = aff_ref[0]
    e, s = aff.shape
    thr = jnp.zeros((e, 1), jnp.int32)
    for bit in range(30, -1, -1):
        cand = thr | (1 << bit)
        cnt = jnp.sum(jnp.where(aff >= pltpu.bitcast(cand, F32), 1.0, 0.0), axis=1, keepdims=True)
        thr = jnp.where(cnt >= cap, cand, thr)
    thr_f = pltpu.bitcast(thr, F32)
    gt = aff > thr_f
    eq = aff == thr_f
    n_gt = jnp.sum(jnp.where(gt, 1.0, 0.0), axis=1, keepdims=True)
    upper = (lax.broadcasted_iota(jnp.int32, (s, s), 0)
             <= lax.broadcasted_iota(jnp.int32, (s, s), 1)).astype(BF16)
    eq_rank = jnp.dot(jnp.where(eq, 1.0, 0.0).astype(BF16), upper, preferred_element_type=F32)
    sel = gt | (eq & (eq_rank <= cap - n_gt))
    sel_f = jnp.where(sel, 1.0, 0.0)
    pos = jnp.dot(sel_f.astype(BF16), upper, preferred_element_type=F32)
    slot_ref[0] = jnp.where(sel, pos - 1.0, -1.0).astype(jnp.int32)


def _route(aff_t, cap):
    b, e, s = aff_t.shape
    return pl.pallas_call(
        functools.partial(_route_body, cap=cap),
        grid=(b,),
        in_specs=[pl.BlockSpec((1, e, s), lambda i: (i, 0, 0))],
        out_specs=pl.BlockSpec((1, e, s), lambda i: (i, 0, 0)),
        out_shape=jax.ShapeDtypeStruct((b, e, s), jnp.int32),
        compiler_params=_params("parallel"),
        name="moe_route",
    )(aff_t)


def _gather_body(h_ref, slot_ref, aff_ref, xe_ref, gate_ref, *, cap):
    e = pl.program_id(1)
    slot = slot_ref[0, pl.ds(e, 1), :]
    aff = aff_ref[0, pl.ds(e, 1), :]
    s = slot.shape[1]
    onehot = lax.broadcasted_iota(jnp.int32, (cap, s), 0) == slot
    xe = jnp.dot(jnp.where(onehot, 1.0, 0.0).astype(BF16), h_ref[0], preferred_element_type=F32)
    xe_ref[0, 0] = xe.astype(xe_ref.dtype)
    gate = jnp.sum(jnp.where(onehot, aff, 0.0), axis=1, keepdims=True)
    gate_ref[0, 0] = jnp.broadcast_to(gate, (cap, HEAD_DIM))


def _gather(h, slot_t, aff_t, cap):
    b, s, d = h.shape
    e = slot_t.shape[1]
    return pl.pallas_call(
        functools.partial(_gather_body, cap=cap),
        grid=(b, e),
        in_specs=[pl.BlockSpec((1, s, d), lambda i, j: (i, 0, 0)),
                  pl.BlockSpec((1, e, s), lambda i, j: (i, 0, 0)),
                  pl.BlockSpec((1, e, s), lambda i, j: (i, 0, 0))],
        out_specs=[pl.BlockSpec((1, 1, cap, d), lambda i, j: (j, i, 0, 0)),
                   pl.BlockSpec((1, 1, cap, HEAD_DIM), lambda i, j: (j, i, 0, 0))],
        out_shape=[jax.ShapeDtypeStruct((e, b, cap, d), BF16),
                   jax.ShapeDtypeStruct((e, b, cap, HEAD_DIM), F32)],
        compiler_params=_params("parallel", "arbitrary"),
        name="moe_gather",
    )(h, slot_t, aff_t)


def _expert_body(x_ref, wg_ref, wu_ref, wd_ref, gate_ref, y_ref, hid_ref, *, nf):
    f = pl.program_id(1)
    nb, cap, d = x_ref.shape[1:]
    tf = hid_ref.shape[2]

    @pl.when(f < nf)
    def _():
        x = x_ref[0].reshape(nb * cap, d)
        hg = jnp.dot(x, wg_ref[0, 0].astype(BF16), preferred_element_type=F32)
        hu = jnp.dot(x, wu_ref[0, 0].astype(BF16), preferred_element_type=F32)
        hid_ref[jnp.minimum(f, nf - 1)] = (hg * jax.nn.sigmoid(hg) * hu).astype(BF16)

    @pl.when(f >= nf)
    def _():
        acc = None
        for c in range(nf):
            part = jnp.dot(hid_ref[c], wd_ref[0, 0, c * tf:(c + 1) * tf, :].astype(BF16), preferred_element_type=F32)
            acc = part if acc is None else acc + part
        gate = gate_ref[0].reshape(nb * cap, HEAD_DIM)[:, :1]
        y_ref[:, 0] = (acc * gate).astype(y_ref.dtype).reshape(nb, cap, acc.shape[1])


def _experts(xe, gates, w_gate, w_up, w_down, layer, tf=512, tn=1024):
    e, b, cap, d = xe.shape
    ff = w_gate.shape[3]
    nf, nd = ff // tf, d // tn
    down = lambda f: jnp.maximum(f - nf, 0)
    up_map = lambda i, f: (layer, jnp.where(f < nf, i, jnp.minimum(i + 1, e - 1)), 0, jnp.where(f < nf, f, 0))
    wd_map = lambda i, f: (layer, jnp.where(f == 0, jnp.maximum(i - 1, 0), i), 0, jnp.where(f == 0, nd - 1, down(f)))
    return pl.pallas_call(
        functools.partial(_expert_body, nf=nf),
        grid=(e, nf + nd),
        in_specs=[pl.BlockSpec((1, b, cap, d), lambda i, f: (i, 0, 0, 0)),
                  pl.BlockSpec((1, 1, d, tf), up_map),
                  pl.BlockSpec((1, 1, d, tf), up_map),
                  pl.BlockSpec((1, 1, ff, tn), wd_map),
                  pl.BlockSpec((1, b, cap, HEAD_DIM), lambda i, f: (i, 0, 0, 0))],
        out_specs=pl.BlockSpec((b, 1, cap, tn), lambda i, f: (0, i, 0, down(f))),
        out_shape=jax.ShapeDtypeStruct((b, e, cap, d), BF16),
        scratch_shapes=[pltpu.VMEM((nf, b * cap, tf), BF16)],
        compiler_params=_params("parallel", "arbitrary"),
        name="moe_experts",
    )(xe, w_gate, w_up, w_down, gates)


def _combine_body(x_ref, slot_ref, ye_ref, g_ref, o_ref, *maybe_h_ref, final_norm):
    ne, cap, d = ye_ref.shape[1:]
    slot = slot_ref[0]
    lane = lax.broadcasted_iota(jnp.int32, (slot.shape[0], cap), 1)
    onehot = jnp.concatenate(
        [jnp.where(slot[:, e:e + 1] == lane, 1.0, 0.0).astype(BF16) for e in range(ne)], axis=-1)
    out = x_ref[0] + jnp.dot(onehot, ye_ref[0].reshape(ne * cap, d), preferred_element_type=F32)
    if final_norm:
        o_ref[0] = _rms(out, g_ref[...])
    else:
        o_ref[0] = out
        maybe_h_ref[0][0] = _rms(out, g_ref[...]).astype(BF16)


def _combine(x, slot_nat, ye, norm_g, final_norm, ts=256):
    b, s, d = x.shape
    _, e, cap, _ = ye.shape
    tile = pl.BlockSpec((1, ts, d), lambda i, j: (i, j, 0))
    outs = ([tile], [jax.ShapeDtypeStruct((b, s, d), F32)]) if final_norm else (
        [tile, tile], [jax.ShapeDtypeStruct((b, s, d), F32), jax.ShapeDtypeStruct((b, s, d), BF16)])
    return pl.pallas_call(
        functools.partial(_combine_body, final_norm=final_norm),
        grid=(b, s // ts),
        in_specs=[pl.BlockSpec((1, ts, d), lambda i, j: (i, j, 0)),
                  pl.BlockSpec((1, ts, e), lambda i, j: (i, j, 0)),
                  pl.BlockSpec((1, e, cap, d), lambda i, j: (i, 0, 0, 0)),
                  pl.BlockSpec((1, d), lambda i, j: (0, 0))],
        out_specs=outs[0],
        out_shape=outs[1],
        compiler_params=_params("parallel", "arbitrary"),
        name="moe_combine",
    )(x, slot_nat, ye, norm_g.reshape(1, d))


def _mixer(x2, h, b, s, layer, w_t, conv_w, a_log, dt_bias, gdn_g, i_bias, f_bias, mlstm_g, w_branch, w_out,
           norm2_g, w_router, rope):
    proj_a = _in_proj(h, w_t, layer, SPAN_A)
    proj_b = _in_proj(h, w_t, layer, SPAN_B)
    proj_c = _in_proj(h, w_t, layer, SPAN_C)
    merge_pre = _in_proj(h, w_t, layer, SPAN_MERGE)
    g_nat, g_t = _gate_proj(h, w_t, layer)
    o_a = _gdn_chunks(_gdn_prep(proj_a.reshape(b, s, -1), conv_w), g_nat, g_t, a_log, dt_bias)
    o_b = _mlstm_chunks(proj_b.reshape(b, s, -1), g_nat, g_t, i_bias, f_bias)
    swa = [_swa_group(proj_c.reshape(b, s, -1), rope[0], rope[1], gi) for gi in range(len(SWA_GROUPS))]
    y = _merge(o_a, o_b, proj_a, proj_b, jnp.tile(gdn_g, N_HEADS).reshape(1, BRANCH_WIDTH),
               mlstm_g.reshape(1, BRANCH_WIDTH), [o for o, _ in swa], [l for _, l in swa])
    mixed = _branch_mix(y, w_branch.astype(BF16), merge_pre)
    x1 = _out_proj(mixed, w_out.astype(BF16), x2)
    return (x1, *_router(x1, norm2_g, w_router))


def _moe(x2, h, aff, b, s, w_gate, w_up, w_down, layer, next_g, final_norm):
    cap = EC_CAPACITY_FACTOR * s // N_EXPERTS
    aff_t = jnp.swapaxes(aff.reshape(b, s, N_EXPERTS), 1, 2)
    slot_t = _route(aff_t, cap)
    xe, gates = _gather(h.reshape(b, s, D_MODEL), slot_t, aff_t, cap)
    ye = _experts(xe, gates, w_gate, w_up, w_down, layer)
    outs = _combine(x2.reshape(b, s, D_MODEL), jnp.swapaxes(slot_t, 1, 2), ye, next_g, final_norm)
    return [o.reshape(b * s, D_MODEL) for o in outs]


def kernel(x, norm1_g, w_in, conv_w, gdn_a_log, gdn_dt_bias, gdn_norm_g, mlstm_i_bias, mlstm_f_bias, mlstm_norm_g, w_branch, w_out, norm2_g, w_router, w_gate, w_up, w_down, final_norm_g):
    b, s, d = x.shape
    rope = _rope_tables(s)
    w_t = jnp.swapaxes(w_in, 1, 2)
    x2 = x.reshape(b * s, d)
    h = _rmsnorm(x2, norm1_g[0], BF16)
    for layer in range(DEPTH):
        final = layer == DEPTH - 1
        x2, h2, aff = _mixer(x2, h, b, s, layer, w_t, conv_w[layer], gdn_a_log[layer], gdn_dt_bias[layer],
                             gdn_norm_g[layer], mlstm_i_bias[layer], mlstm_f_bias[layer], mlstm_norm_g[layer],
                             w_branch[layer], w_out[layer], norm2_g[layer], w_router[layer], rope)
        outs = _moe(x2, h2, aff, b, s, w_gate, w_up, w_down, layer,
                    final_norm_g if final else norm1_g[layer + 1], final)
        x2, h = (outs[0], None) if final else outs
    return x2.reshape(b, s, d)
```
